```python
import math
import jax
import jax.numpy as jnp
from jax import lax
import numpy as np

D_MODEL = 1024
BATCH = 8
SEQ = 4096
DEPTH = 2

CTX_LEN = 256
GRID_W = 64
N_BRANCH = 4
BR_W = 512
EPS = 1e-6
CONV_W = 4
DT_MIN = 1e-3
DT_MAX = 1e-1

HG_HEADS = 4
HG_DK = BR_W // HG_HEADS
HG_CHUNK = 64
S5_GROUP = 16
S5_GROUPS = BR_W // S5_GROUP
S5_STATE = 64
LRU_BLOCKS = 8
LRU_BW = BR_W // LRU_BLOCKS
LRU_C = 8.0
M2_HEADDIM = 64
M2_HEADS = BR_W // M2_HEADDIM
M2_GROUPS = 2
M2_HPG = M2_HEADS // M2_GROUPS
M2_STATE = 64
M2_CHUNK = 64
M2_XBC = BR_W + 2 * M2_GROUPS * M2_STATE

IN_SIZES = (BR_W,) * 9 + (M2_XBC, 2 * M2_HEADS, BR_W)
IN_SPLITS = tuple(sum(IN_SIZES[:i + 1]) for i in range(len(IN_SIZES) - 1))
IN_COLS = sum(IN_SIZES)

kernel_name = 'hybrid_gated_recurrent_flow_block'


def _rms(x, w):
    xf = x.astype(jnp.float32)
    y = xf * lax.rsqrt(jnp.mean(xf * xf, axis=-1, keepdims=True) + EPS)
    return (y * w.astype(jnp.float32)).astype(x.dtype)


def _lin_comb(e1, e2):
    a1, b1 = e1
    a2, b2 = e2
    return a1 * a2, a2 * b1 + b2


def _cat_f(c, l):
    return jnp.concatenate([c, l], axis=1)


def _cat_b(c, l):
    return jnp.flip(jnp.concatenate([l, c], axis=1), axis=1)


def _uncat_b(y, n_ctx):
    y = jnp.flip(y, axis=1)
    n_lat = y.shape[1] - n_ctx
    return jnp.concatenate([y[:, n_lat:], y[:, :n_lat]], axis=1)


def _f_to_b(y, n_ctx):
    return _cat_b(y[:, :n_ctx], y[:, n_ctx:])


def _dwconv(x, w, b):
    n = x.shape[-2]
    lo = (CONV_W - 1) // 2
    pad = [(0, 0)] * (x.ndim - 2) + [(lo, CONV_W - 1 - lo), (0, 0)]
    xp = jnp.pad(x, pad)
    out = b
    for k in range(CONV_W):
        out = out + w[k] * xp[..., k:k + n, :]
    return out


def _short_conv(xc, xl, w, b):
    bsz, n_lat, ch = xl.shape
    rows = n_lat // GRID_W
    yl = _dwconv(xl.reshape(bsz, rows, GRID_W, ch), w, b).reshape(bsz, n_lat, ch)
    return _dwconv(xc, w, b), yl


def _gla_chunked(q, k, v, logf):
    bsz, T, H, _ = q.shape
    n = T // HG_CHUNK

    def r(a):
        return a.reshape(bsz, n, HG_CHUNK, H, a.shape[-1])
    q, k, v, logf = r(q), r(k), r(v), r(logf)
    b = jnp.cumsum(logf, axis=2)
    b_end = b[:, :, -1:]
    mid = 0.5 * b_end
    att = jnp.einsum('bnihk,bnjhk->bnhij', q * jnp.exp(b - mid), k * jnp.exp(mid - b))
    mask = jnp.tril(jnp.ones((HG_CHUNK, HG_CHUNK), bool))
    att = jnp.where(mask, att, 0.0)
    o_intra = jnp.einsum('bnhij,bnjhv->bnihv', att, v)
    chunk_kv = jnp.einsum('bnjhk,bnjhv->bnhkv', k * jnp.exp(b_end - b), v)
    decay = jnp.exp(b_end[:, :, 0])

    def step(s, inp):
        kv, dec = inp
        return dec[..., None] * s + kv, s
    s0 = jnp.zeros((bsz, H, k.shape[-1], v.shape[-1]), jnp.float32)
    _, s_prev = lax.scan(step, s0, (jnp.moveaxis(chunk_kv, 1, 0), jnp.moveaxis(decay, 1, 0)))
    s_prev = jnp.moveaxis(s_prev, 0, 1)
    o_inter = jnp.einsum('bnihk,bnhkv->bnihv', q * jnp.exp(b), s_prev)
    return (o_intra + o_inter).reshape(bsz, T, H, v.shape[-1])


def _hgrn2(ctx_in, lat_in, lb, norm_w):
    (cq, ci, cff, cfb, cz), (lq, li, lff, lfb, lz) = ctx_in, lat_in
    n_ctx = cq.shape[1]
    f32 = jnp.float32

    def heads(a):
        return a.reshape(a.shape[0], a.shape[1], HG_HEADS, HG_DK)

    def run(q, i, f_raw, lbd):
        f = lbd + (1.0 - lbd) * jax.nn.sigmoid(f_raw.astype(f32))
        return _gla_chunked(heads(jax.nn.silu(q.astype(f32))), heads(1.0 - f),
                            heads(i.astype(f32)), heads(jnp.log(f)))
    o = (run(_cat_f(cq, lq), _cat_f(ci, li), _cat_f(cff, lff), lb[0])
         + _uncat_b(run(_cat_b(cq, lq), _cat_b(ci, li), _cat_b(cfb, lfb), lb[1]), n_ctx))
    o = _rms(o, norm_w.reshape(HG_HEADS, HG_DK))
    o = o.reshape(o.shape[0], o.shape[1], BR_W)
    return o * jax.nn.silu(_cat_f(cz, lz).astype(f32))


def _s5(ctx_in, lat_in, a_re, a_im, log_step, b_re, b_im, c_re, c_im, d_skip, w_glu, b_glu):
    (cu, cz), (lu, lz) = ctx_in, lat_in
    n_ctx = cu.shape[1]
    f32 = jnp.float32
    u = _cat_f(cu, lu).astype(f32)
    bsz, T, _ = u.shape
    ug = u.reshape(bsz, T, S5_GROUPS, S5_GROUP).astype(jnp.complex64)
    b_mat = lax.complex(b_re.astype(f32), b_im.astype(f32))
    c_mat = lax.complex(c_re.astype(f32), c_im.astype(f32))
    bu = jnp.einsum('gnp,btgp->btgn', b_mat, ug)

    def run(bu_seq, d):
        lam = lax.complex(a_re[d].astype(f32), a_im[d].astype(f32))
        step = jnp.exp(log_step[d].astype(f32))[:, None]
        a_bar = jnp.exp(lam * step)
        drive = ((a_bar - 1.0) / lam) * bu_seq
        a_seq = jnp.broadcast_to(a_bar, (1, T) + a_bar.shape)
        _, s = lax.associative_scan(_lin_comb, (a_seq, drive), axis=1)
        return jnp.real(jnp.einsum('gpn,btgn->btgp', c_mat, s))
    y = run(bu, 0) + _uncat_b(run(_f_to_b(bu, n_ctx), 1), n_ctx)
    y = y.reshape(bsz, T, BR_W) + d_skip.astype(f32) * u
    g = jax.nn.gelu(y)
    y = g * jax.nn.sigmoid(g @ w_glu.astype(f32) + b_glu.astype(f32))
    return y * jax.nn.silu(_cat_f(cz, lz).astype(f32))


def _rglru(ctx_in, lat_in, conv_w, conv_b, gate_w, gate_b, lam):
    (cx, cz), (lx, lz) = ctx_in, lat_in
    n_ctx = cx.shape[1]
    f32 = jnp.float32
    cx, lx = _short_conv(cx, lx, conv_w, conv_b)

    def run(xs, d):
        xf = xs.astype(f32)
        bsz, T, _ = xf.shape
        xb = xf.reshape(bsz, T, LRU_BLOCKS, LRU_BW)
        gates = jax.nn.sigmoid(jnp.einsum('btnc,gncd->gbtnd', xb, gate_w[d].astype(f32))
                               + gate_b[d].astype(f32)[:, None, None])
        r = gates[0].reshape(bsz, T, BR_W)
        i = gates[1].reshape(bsz, T, BR_W)
        log_a = -LRU_C * r * jax.nn.softplus(-lam[d].astype(f32))
        b = jnp.sqrt(-jnp.expm1(2.0 * log_a)) * (i * xf)
        _, h = lax.associative_scan(_lin_comb, (jnp.exp(log_a), b), axis=1)
        return h
    h = run(_cat_f(cx, lx), 0) + _uncat_b(run(_cat_b(cx, lx), 1), n_ctx)
    return h * jax.nn.silu(_cat_f(cz, lz).astype(f32))


def _ssd_chunked(x, dt, a, bm, cm):
    bsz, T = x.shape[:2]
    n = T // M2_CHUNK

    def r(t):
        return t.reshape((bsz, n, M2_CHUNK) + t.shape[2:])
    x, dt, bm, cm = r(x), r(dt), r(bm), r(cm)
    cum = jnp.cumsum(dt * a, axis=2)
    seg = cum[:, :, :, None] - cum[:, :, None, :]
    mask = jnp.tril(jnp.ones((M2_CHUNK, M2_CHUNK), bool))[:, :, None, None]
    decay_ij = jnp.exp(jnp.where(mask, seg, -jnp.inf))
    scores = jnp.einsum('bcigs,bcjgs->bcijg', cm, bm)
    w = scores[..., None] * decay_ij * dt[:, :, None]
    y_intra = jnp.einsum('bcijgr,bcjgrp->bcigrp', w, x)
    cum_end = cum[:, :, -1]
    wx = (jnp.exp(cum_end[:, :, None] - cum) * dt)[..., None] * x
    chunk_state = jnp.einsum('bcjgs,bcjgrp->bcgrps', bm, wx)

    def step(s, inp):
        st, dec = inp
        return dec[..., None, None] * s + st, s
    s0 = jnp.zeros(chunk_state.shape[:1] + chunk_state.shape[2:], jnp.float32)
    _, s_prev = lax.scan(step, s0, (jnp.moveaxis(chunk_state, 1, 0), jnp.moveaxis(jnp.exp(cum_end), 1, 0)))
    s_prev = jnp.moveaxis(s_prev, 0, 1)
    y_inter = jnp.einsum('bcigs,bcgrps->bcigrp', cm, s_prev) * jnp.exp(cum)[..., None]
    return (y_intra + y_inter).reshape(bsz, T, M2_GROUPS, M2_HPG, M2_HEADDIM)


def _mamba2(ctx_in, lat_in, conv_w, conv_b, dt_bias, a_log, d_skip, norm_w):
    (cxbc, cdt, cz), (lxbc, ldt, lz) = ctx_in, lat_in
    n_ctx = cxbc.shape[1]
    f32 = jnp.float32
    cxbc, lxbc = _short_conv(cxbc, lxbc, conv_w, conv_b)
    cxbc, lxbc = jax.nn.silu(cxbc.astype(f32)), jax.nn.silu(lxbc.astype(f32))
    gn = M2_GROUPS * M2_STATE

    def run(xbc, dt_raw, d):
        bsz, T, _ = xbc.shape
        xs = xbc[..., :BR_W].reshape(bsz, T, M2_GROUPS, M2_HPG, M2_HEADDIM)
        bm = xbc[..., BR_W:BR_W + gn].reshape(bsz, T, M2_GROUPS, M2_STATE)
        cm = xbc[..., BR_W + gn:].reshape(bsz, T, M2_GROUPS, M2_STATE)
        dt = jax.nn.softplus(dt_raw.astype(f32) + dt_bias[d].astype(f32)).reshape(bsz, T, M2_GROUPS, M2_HPG)
        a = -jnp.exp(a_log[d].astype(f32)).reshape(M2_GROUPS, M2_HPG)
        return _ssd_chunked(xs, dt, a, bm, cm)
    xbc_f = _cat_f(cxbc, lxbc)
    y = (run(xbc_f, _cat_f(cdt[..., :M2_HEADS], ldt[..., :M2_HEADS]), 0)
         + _uncat_b(run(_cat_b(cxbc, lxbc), _cat_b(cdt[..., M2_HEADS:], ldt[..., M2_HEADS:]), 1), n_ctx))
    bsz, T, _ = xbc_f.shape
    skip = d_skip.astype(f32)[:, None] * xbc_f[..., :BR_W].reshape(bsz, T, M2_HEADS, M2_HEADDIM)
    y = y.reshape(bsz, T, BR_W) + skip.reshape(bsz, T, BR_W)
    return _rms(y * jax.nn.silu(_cat_f(cz, lz).astype(f32)), norm_w)


def _merge(h, ys, w_gate, b_gate, w_branch):
    out = jax.nn.sigmoid(h @ w_gate[0] + b_gate[0]) * (ys[0].astype(h.dtype) @ w_branch[0])
    for k in range(1, N_BRANCH):
        out = out + jax.nn.sigmoid(h @ w_gate[k] + b_gate[k]) * (ys[k].astype(h.dtype) @ w_branch[k])
    return out


def setup_inputs(seed: int = 0) -> dict:
    key = jax.random.key(seed)
    ks = iter(jax.random.split(key, 48))
    f32 = jnp.float32

    def nrm(shape, s=1.0):
        return s * jax.random.normal(next(ks), shape, f32)

    def uni(shape, lo, hi):
        return jax.random.uniform(next(ks), shape, f32, lo, hi)
    L = DEPTH
    dt_m2 = jnp.exp(uni((L, 2, M2_HEADS), math.log(DT_MIN), math.log(DT_MAX)))
    lru_a = uni((L, 2, BR_W), 0.9, 0.999) ** (1.0 / LRU_C)
    return {
        'x': nrm((BATCH, SEQ, D_MODEL)),
        'c': nrm((BATCH, D_MODEL)),
        'ctx': nrm((BATCH, CTX_LEN, D_MODEL)),
        'c_ctx': nrm((D_MODEL,)),
        'norm_w': 1.0 + nrm((L, D_MODEL), 0.02),
        'w_mod': nrm((L, D_MODEL, 3 * D_MODEL), 0.5 * D_MODEL ** -0.5),
        'b_mod': nrm((L, 3 * D_MODEL), 0.01),
        'w_in': nrm((L, D_MODEL, IN_COLS), D_MODEL ** -0.5),
        'hg_lb_logits': nrm((L + 1, 2, BR_W), 0.1),
        'hg_norm': 1.0 + nrm((L, BR_W), 0.02),
        's5_a_re': -0.5 + nrm((L, 2, S5_GROUPS, S5_STATE), 0.01),
        's5_a_im': math.pi * jnp.arange(S5_STATE, dtype=f32) + nrm((L, 2, S5_GROUPS, S5_STATE), 0.01),
        's5_log_step': uni((L, 2, S5_GROUPS), math.log(DT_MIN), math.log(DT_MAX)),
        's5_b_re': nrm((L, S5_GROUPS, S5_STATE, S5_GROUP), (2 * S5_GROUP) ** -0.5),
        's5_b_im': nrm((L, S5_GROUPS, S5_STATE, S5_GROUP), (2 * S5_GROUP) ** -0.5),
        's5_c_re': nrm((L, S5_GROUPS, S5_GROUP, S5_STATE), S5_STATE ** -0.5),
        's5_c_im': nrm((L, S5_GROUPS, S5_GROUP, S5_STATE), S5_STATE ** -0.5),
        's5_d': nrm((L, BR_W)),
        's5_w_glu': nrm((L, BR_W, BR_W), BR_W ** -0.5),
        's5_b_glu': nrm((L, BR_W), 0.01),
        'lru_conv_w': nrm((L, CONV_W, BR_W), CONV_W ** -0.5),
        'lru_conv_b': nrm((L, BR_W), 0.01),
        'lru_gate_w': nrm((L, 2, 2, LRU_BLOCKS, LRU_BW, LRU_BW), LRU_BW ** -0.5),
        'lru_gate_b': nrm((L, 2, 2, LRU_BLOCKS, LRU_BW), 0.01),
        'lru_lam': jnp.log(lru_a) - jnp.log1p(-lru_a),
        'm2_conv_w': nrm((L, CONV_W, M2_XBC), CONV_W ** -0.5),
        'm2_conv_b': nrm((L, M2_XBC), 0.01),
        'm2_dt_bias': dt_m2 + jnp.log(-jnp.expm1(-dt_m2)),
        'm2_a_log': jnp.log(uni((L, 2, M2_HEADS), 1.0, 16.0)),
        'm2_d': 1.0 + nrm((L, M2_HEADS), 0.02),
        'm2_norm': 1.0 + nrm((L, BR_W), 0.02),
        'w_branch': nrm((L, N_BRANCH, BR_W, D_MODEL), BR_W ** -0.5),
        'w_gate': nrm((L, N_BRANCH, D_MODEL, D_MODEL), D_MODEL ** -0.5),
        'b_gate': nrm((L, N_BRANCH, D_MODEL), 0.01),
        'w_out': nrm((L, D_MODEL, D_MODEL), D_MODEL ** -0.5),
        'final_norm': 1.0 + nrm((D_MODEL,), 0.02),
    }


def reference(x, c, ctx, c_ctx, norm_w, w_mod, b_mod, w_in, hg_lb_logits, hg_norm,
              s5_a_re, s5_a_im, s5_log_step, s5_b_re, s5_b_im, s5_c_re, s5_c_im, s5_d, s5_w_glu, s5_b_glu,
              lru_conv_w, lru_conv_b, lru_gate_w, lru_gate_b, lru_lam,
              m2_conv_w, m2_conv_b, m2_dt_bias, m2_a_log, m2_d, m2_norm,
              w_branch, w_gate, b_gate, w_out, final_norm):
    n_ctx = ctx.shape[1]
    lb_all = jnp.cumsum(jax.nn.softmax(hg_lb_logits.astype(jnp.float32), axis=0), axis=0)
    for l in range(DEPTH):
        mod = jax.nn.silu(c) @ w_mod[l] + b_mod[l]
        mod_c = jax.nn.silu(c_ctx) @ w_mod[l] + b_mod[l]
        sh, sc, gt = jnp.split(mod[:, None, :], 3, axis=-1)
        sh_c, sc_c, gt_c = jnp.split(mod_c, 3, axis=-1)
        h = _rms(x, norm_w[l]) * (1.0 + sc) + sh
        hc = _rms(ctx, norm_w[l]) * (1.0 + sc_c) + sh_c
        u = jnp.split(h @ w_in[l], IN_SPLITS, axis=-1)
        uc = jnp.split(hc @ w_in[l], IN_SPLITS, axis=-1)
        ys = (
            _hgrn2(uc[0:5], u[0:5], lb_all[l], hg_norm[l]),
            _s5(uc[5:7], u[5:7], s5_a_re[l], s5_a_im[l], s5_log_step[l], s5_b_re[l], s5_b_im[l],
                s5_c_re[l], s5_c_im[l], s5_d[l], s5_w_glu[l], s5_b_glu[l]),
            _rglru(uc[7:9], u[7:9], lru_conv_w[l], lru_conv_b[l], lru_gate_w[l], lru_gate_b[l], lru_lam[l]),
            _mamba2(uc[9:12], u[9:12], m2_conv_w[l], m2_conv_b[l], m2_dt_bias[l], m2_a_log[l], m2_d[l], m2_norm[l]),
        )
        x_new = x + gt * (_merge(h, [y[:, n_ctx:] for y in ys], w_gate[l], b_gate[l], w_branch[l]) @ w_out[l])
        if l < DEPTH - 1:
            ctx = ctx + gt_c * (_merge(hc, [y[:, :n_ctx] for y in ys], w_gate[l], b_gate[l], w_branch[l]) @ w_out[l])
        x = x_new
    return _rms(x, final_norm)
```

```python
import functools
import math

import jax
import jax.numpy as jnp
from jax import lax
from jax.experimental import pallas as pl
from jax.experimental.pallas import tpu as pltpu

F32 = jnp.float32
BF16 = jnp.bfloat16
HIGHEST = lax.Precision.HIGHEST

NB = 8
EPS = 1e-6
CHUNK = 64
CONV_W = 4
BR_W = 512
HG_HEADS = 4
HG_DK = BR_W // HG_HEADS
S5_GROUP = 16
S5_STATE = 64
S5_NCH = 4
LRU_C = 8.0
M2_HEADDIM = 64
M2_HEADS = BR_W // M2_HEADDIM
M2_GROUPS = 2
M2_STATE = 64
M2_XBC = BR_W + 2 * M2_GROUPS * M2_STATE
DT_PAD = 128
VMEM_LIMIT = 56 * 1024 * 1024


def _cp(n_axes):
    return pltpu.CompilerParams(dimension_semantics=("arbitrary",) * n_axes,
                                vmem_limit_bytes=VMEM_LIMIT)


def _dot(a, b):
    return jnp.dot(a, b, preferred_element_type=F32)


def _dot_hi(a, b):
    return jnp.dot(a, b, precision=HIGHEST, preferred_element_type=F32)


def _dot_nt(a, b):
    return lax.dot_general(a, b, (((1,), (1,)), ((), ())), preferred_element_type=F32)


def _dot_tn(a, b):
    return lax.dot_general(a, b, (((0,), (0,)), ((), ())), preferred_element_type=F32)


def _sigmoid(x):
    return 1.0 / (1.0 + jnp.exp(-x))


def _silu(x):
    return x * _sigmoid(x)


def _softplus(x):
    return jnp.maximum(x, 0.0) + jnp.log1p(jnp.exp(-jnp.abs(x)))


def _adaln(x, nw, sc, sh):
    tm, d = x.shape
    ms = jnp.mean(x * x, axis=-1, keepdims=True)
    xn = (x * lax.rsqrt(ms + EPS)) * nw
    xn3 = xn.reshape(tm // NB, NB, d)
    h = xn3 * (1.0 + sc)[None] + sh[None]
    return h.reshape(tm, d)


def _mod_kernel(c_ref, w_ref, b_ref, o_ref):
    s = _silu(c_ref[...]).astype(BF16)
    o_ref[...] = _dot(s, w_ref[...]) + b_ref[...]


def _mod_call(c_all, w, b):
    n, d = c_all.shape
    m = w.shape[1]
    return pl.pallas_call(
        _mod_kernel,
        out_shape=jax.ShapeDtypeStruct((n, m), F32),
        grid=(1,),
        in_specs=[pl.BlockSpec((n, d), lambda i: (0, 0)),
                  pl.BlockSpec((d, m), lambda i: (0, 0)),
                  pl.BlockSpec((1, m), lambda i: (0, 0))],
        out_specs=pl.BlockSpec((n, m), lambda i: (0, 0)),
        compiler_params=_cp(1),
        name="mod",
    )(c_all, w, b)


def _proj_kernel(x_ref, nw_ref, sc_ref, sh_ref, w_ref, *o_refs, widths):
    hb = _adaln(x_ref[...], nw_ref[...], sc_ref[...], sh_ref[...]).astype(BF16)
    off = 0
    for o_ref, wd in zip(o_refs, widths):
        o_ref[...] = _dot(hb, w_ref[:, off:off + wd])
        off += wd


def _proj_call(xr, nw, sc, sh, w, widths, name):
    r, d = xr.shape
    n = w.shape[1]
    tm = min(512, r)
    return pl.pallas_call(
        functools.partial(_proj_kernel, widths=widths),
        out_shape=[jax.ShapeDtypeStruct((r, wd), F32) for wd in widths],
        grid=(r // tm,),
        in_specs=[pl.BlockSpec((tm, d), lambda i: (i, 0)),
                  pl.BlockSpec((1, d), lambda i: (0, 0)),
                  pl.BlockSpec((NB, d), lambda i: (0, 0)),
                  pl.BlockSpec((NB, d), lambda i: (0, 0)),
                  pl.BlockSpec((d, n), lambda i: (0, 0))],
        out_specs=[pl.BlockSpec((tm, wd), lambda i: (i, 0)) for wd in widths],
        compiler_params=_cp(1),
        name=name,
    )(xr, nw, sc, sh, w)


def _merge_kernel(x_ref, ya_ref, yb_ref, yc_ref, yd_ref, nw_ref, sc_ref, sh_ref, gt_ref,
                  wg_ref, bg_ref, wb_ref, wo_ref, fn_ref, o_ref, *, final):
    x = x_ref[...]
    tm, d = x.shape
    hb = _adaln(x, nw_ref[...], sc_ref[...], sh_ref[...]).astype(BF16)
    m = None
    for k, y_ref in enumerate((ya_ref, yb_ref, yc_ref, yd_ref)):
        g = _sigmoid(_dot(hb, wg_ref[k]) + bg_ref[k])
        p = _dot(y_ref[...].astype(BF16), wb_ref[k])
        m = g * p if m is None else m + g * p
    upd = _dot(m.astype(BF16), wo_ref[...])
    out = x + (upd.reshape(tm // NB, NB, d) * gt_ref[...][None]).reshape(tm, d)
    if final:
        ms = jnp.mean(out * out, axis=-1, keepdims=True)
        out = (out * lax.rsqrt(ms + EPS)) * fn_ref[...]
    o_ref[...] = out


def _merge_call(xr, ys, nw, sc, sh, gt, wg, bg, wb, wo, fn, final, name):
    r, d = xr.shape
    tm = min(512, r)
    row = lambda i: (i, 0)
    c2 = lambda i: (0, 0)
    c3 = lambda i: (0, 0, 0)
    return pl.pallas_call(
        functools.partial(_merge_kernel, final=final),
        out_shape=jax.ShapeDtypeStruct((r, d), F32),
        grid=(r // tm,),
        in_specs=[pl.BlockSpec((tm, d), row)]
                 + [pl.BlockSpec((tm, BR_W), row)] * 4
                 + [pl.BlockSpec((1, d), c2), pl.BlockSpec((NB, d), c2),
                    pl.BlockSpec((NB, d), c2), pl.BlockSpec((NB, d), c2),
                    pl.BlockSpec(wg.shape, c3), pl.BlockSpec(bg.shape, c3),
                    pl.BlockSpec(wb.shape, c3), pl.BlockSpec(wo.shape, c2),
                    pl.BlockSpec((1, d), c2)],
        out_specs=pl.BlockSpec((tm, d), row),
        compiler_params=_cp(1),
        name=name,
    )(xr, *ys, nw, sc, sh, gt, wg, bg, wb, wo, fn)


def _conv_tap_tm(x_ref, j, off, period, tt, width):
    s0 = CHUNK * j + off
    plo = (CHUNK * j // period) * period
    phi = min(plo + period, tt)
    lo, hi = max(s0, plo), min(s0 + CHUNK, phi)
    parts = []
    if lo - s0 > 0:
        parts.append(jnp.zeros(((lo - s0) * NB, width), F32))
    parts.append(x_ref[lo * NB:hi * NB, 0:width])
    if s0 + CHUNK - hi > 0:
        parts.append(jnp.zeros(((s0 + CHUNK - hi) * NB, width), F32))
    return parts[0] if len(parts) == 1 else jnp.concatenate(parts, axis=0)


def _lru_kernel(*refs, reverse, post, period, tt):
    if post:
        (u_ref, of_ref, st_in_ref, cw_ref, cb_ref, wg_ref, bg_ref, lam_ref,
         o_ref, st_ref, a_scr, b_scr) = refs
    else:
        (u_ref, st_in_ref, cw_ref, cb_ref, wg_ref, bg_ref, lam_ref,
         o_ref, st_ref, a_scr, b_scr) = refs
        of_ref = None

    @pl.when(pl.program_id(0) == 0)
    def _():
        st_ref[...] = st_in_ref[...]

    nsub = tt // CHUNK
    sp = _softplus(-lam_ref[...])
    rows_sub = CHUNK * NB
    order = range(nsub - 1, -1, -1) if reverse else range(nsub)
    for j in order:
        xc = cb_ref[...] + cw_ref[1:2, :] * _conv_tap_tm(u_ref, j, 0, period, tt, BR_W)
        for k in (0, 2, 3):
            xc = xc + cw_ref[k:k + 1, :] * _conv_tap_tm(u_ref, j, k - 1, period, tt, BR_W)
        gates = _sigmoid(_dot(xc.astype(BF16), wg_ref[...]) + bg_ref[...])
        r = gates[:, 0:BR_W]
        ig = gates[:, BR_W:2 * BR_W]
        log_a = (-LRU_C) * r * sp
        a = jnp.exp(log_a)
        a_scr[...] = a
        b_scr[...] = jnp.sqrt(-jnp.tanh(log_a) * (a * a + 1.0)) * (ig * xc)

        def body(i, h):
            t = (CHUNK - 1 - i) if reverse else i
            r0 = pl.multiple_of(t * NB, NB)
            h = a_scr[pl.ds(r0, NB), :] * h + b_scr[pl.ds(r0, NB), :]
            b_scr[pl.ds(r0, NB), :] = h
            return h

        st_ref[...] = lax.fori_loop(0, CHUNK, body, st_ref[...], unroll=8)
        rs = slice(j * rows_sub, (j + 1) * rows_sub)
        if post:
            z = u_ref[rs, BR_W:2 * BR_W]
            o_ref[rs, :] = (of_ref[rs, :] + b_scr[...]) * _silu(z)
        else:
            o_ref[rs, :] = b_scr[...]


def _lru_call(u, of, st_in, cw, cb, wg, bg, lam, reverse, period, name):
    r = u.shape[0]
    tp = r // NB
    tt = tp if period != CHUNK else CHUNK
    nblk = tp // tt
    rows = tt * NB
    post = of is not None
    tmap = (lambda i: (nblk - 1 - i, 0)) if reverse else (lambda i: (i, 0))
    c2 = lambda i: (0, 0)
    in_specs = [pl.BlockSpec((rows, 2 * BR_W), tmap)]
    args = [u]
    if post:
        in_specs.append(pl.BlockSpec((rows, BR_W), tmap))
        args.append(of)
    in_specs += [pl.BlockSpec((NB, BR_W), c2), pl.BlockSpec((CONV_W, BR_W), c2),
                 pl.BlockSpec((1, BR_W), c2), pl.BlockSpec((BR_W, 2 * BR_W), c2),
                 pl.BlockSpec((1, 2 * BR_W), c2), pl.BlockSpec((1, BR_W), c2)]
    args += [st_in, cw, cb, wg, bg, lam]
    return pl.pallas_call(
        functools.partial(_lru_kernel, reverse=reverse, post=post, period=period, tt=tt),
        out_shape=[jax.ShapeDtypeStruct((r, BR_W), F32),
                   jax.ShapeDtypeStruct((NB, BR_W), F32)],
        grid=(nblk,),
        in_specs=in_specs,
        out_specs=[pl.BlockSpec((rows, BR_W), tmap), pl.BlockSpec((NB, BR_W), c2)],
        scratch_shapes=[pltpu.VMEM((CHUNK * NB, BR_W), F32),
                        pltpu.VMEM((CHUNK * NB, BR_W), F32)],
        compiler_params=_cp(1),
        name=name,
    )(*args)


def _gelu_tanh(x):
    return 0.5 * x * (1.0 + jnp.tanh(math.sqrt(2.0 / math.pi) * (x + 0.044715 * (x * x * x))))


def _s5_kernel(*refs, reverse, post):
    if post:
        (u_ref, of_ref, st_in_ref, bm_ref, ab_ref, cm_ref, dsk_ref, wglu_ref, bglu_ref,
         o_ref, st_ref, s_scr, y_scr) = refs
    else:
        (u_ref, st_in_ref, bm_ref, ab_ref, cm_ref, o_ref, st_ref, s_scr) = refs

    @pl.when(pl.program_id(0) == 0)
    def _():
        st_ref[...] = st_in_ref[...]

    half = (BR_W // S5_GROUP) * S5_STATE // S5_NCH
    cin = BR_W // S5_NCH
    for c in range(S5_NCH):
        ub = u_ref[:, c * cin:(c + 1) * cin].astype(BF16)
        s_scr[...] = _dot(ub, bm_ref[c])
        a_re = ab_ref[c, :, 0:half]
        a_im = ab_ref[c, :, half:2 * half]
        c0 = c * 2 * half

        def body(i, carry):
            s_re, s_im = carry
            t = (CHUNK - 1 - i) if reverse else i
            r0 = pl.multiple_of(t * NB, NB)
            n_re = a_re * s_re - a_im * s_im + s_scr[pl.ds(r0, NB), 0:half]
            n_im = a_re * s_im + a_im * s_re + s_scr[pl.ds(r0, NB), half:2 * half]
            s_scr[pl.ds(r0, NB), 0:half] = n_re
            s_scr[pl.ds(r0, NB), half:2 * half] = n_im
            return n_re, n_im

        s_re, s_im = lax.fori_loop(
            0, CHUNK, body,
            (st_ref[:, c0:c0 + half], st_ref[:, c0 + half:c0 + 2 * half]), unroll=4)
        st_ref[:, c0:c0 + half] = s_re
        st_ref[:, c0 + half:c0 + 2 * half] = s_im
        yc = _dot(s_scr[...].astype(BF16), cm_ref[c])
        if post:
            y_scr[:, c * cin:(c + 1) * cin] = yc
        else:
            o_ref[:, c * cin:(c + 1) * cin] = yc
    if post:
        u = u_ref[:, 0:BR_W]
        z = u_ref[:, BR_W:2 * BR_W]
        y = of_ref[...] + y_scr[...] + dsk_ref[...] * u
        g = _gelu_tanh(y)
        gl = _dot(g.astype(BF16), wglu_ref[...]) + bglu_ref[...]
        o_ref[...] = g * _sigmoid(gl) * _silu(z)


def _s5_call(u, of, st_in, bm, ab, cm, dsk, wglu, bglu, reverse, name):
    r = u.shape[0]
    rows = CHUNK * NB
    nblk = r // rows
    post = of is not None
    nst = st_in.shape[1]
    tmap = (lambda i: (nblk - 1 - i, 0)) if reverse else (lambda i: (i, 0))
    c2 = lambda i: (0, 0)
    c3 = lambda i: (0, 0, 0)
    in_specs = [pl.BlockSpec((rows, 2 * BR_W), tmap)]
    args = [u]
    if post:
        in_specs.append(pl.BlockSpec((rows, BR_W), tmap))
        args.append(of)
    in_specs += [pl.BlockSpec((NB, nst), c2), pl.BlockSpec(bm.shape, c3),
                 pl.BlockSpec(ab.shape, c3), pl.BlockSpec(cm.shape, c3)]
    args += [st_in, bm, ab, cm]
    scratch = [pltpu.VMEM((rows, nst // S5_NCH), F32)]
    if post:
        in_specs += [pl.BlockSpec((1, BR_W), c2), pl.BlockSpec((BR_W, BR_W), c2),
                     pl.BlockSpec((1, BR_W), c2)]
        args += [dsk, wglu, bglu]
        scratch.append(pltpu.VMEM((rows, BR_W), F32))
    return pl.pallas_call(
        functools.partial(_s5_kernel, reverse=reverse, post=post),
        out_shape=[jax.ShapeDtypeStruct((r, BR_W), F32),
                   jax.ShapeDtypeStruct((NB, nst), F32)],
        grid=(nblk,),
        in_specs=in_specs,
        out_specs=[pl.BlockSpec((rows, BR_W), tmap), pl.BlockSpec((NB, nst), c2)],
        scratch_shapes=scratch,
        compiler_params=_cp(1),
        name=name,
    )(*args)


def _tri_mask(n, reverse):
    i = lax.broadcasted_iota(jnp.int32, (n, n), 0)
    j = lax.broadcasted_iota(jnp.int32, (n, n), 1)
    return (j >= i) if reverse else (j <= i)


def _hgrn_kernel(*refs, reverse, post, tb):
    if post:
        (q_ref, v_ref, f_ref, z_ref, of_ref, lb_ref, nrm_ref, st_in_ref,
         o_ref, st_ref, o_scr) = refs
    else:
        (q_ref, v_ref, f_ref, lb_ref, st_in_ref, o_ref, st_ref) = refs

    @pl.when(pl.program_id(1) == 0)
    def _():
        st_ref[...] = st_in_ref[...]

    mask = _tri_mask(CHUNK, reverse)
    tri = mask.astype(F32)
    lb = lb_ref[...]
    nchunk = tb // CHUNK
    order = range(nchunk - 1, -1, -1) if reverse else range(nchunk)
    for c in order:
        rs = slice(c * CHUNK, (c + 1) * CHUNK)
        f = lb + (1.0 - lb) * _sigmoid(f_ref[rs, :])
        k = 1.0 - f
        qs = _silu(q_ref[rs, :])
        b = _dot_hi(tri, jnp.log(f))
        b_end = b[0:1, :] if reverse else b[CHUNK - 1:CHUNK, :]
        mid = 0.5 * b_end
        qt = (qs * jnp.exp(b - mid)).astype(BF16)
        kt = (k * jnp.exp(mid - b)).astype(BF16)
        qb = (qs * jnp.exp(b)).astype(BF16)
        kb = (k * jnp.exp(b_end - b)).astype(BF16)
        dec = jnp.exp(b_end)
        vb = v_ref[rs, :].astype(BF16)
        for h in range(HG_HEADS):
            sl = slice(h * HG_DK, (h + 1) * HG_DK)
            att = jnp.where(mask, _dot_nt(qt[:, sl], kt[:, sl]), 0.0)
            st = st_ref[h]
            o = _dot(att.astype(BF16), vb[:, sl]) + _dot_nt(qb[:, sl], st.astype(BF16))
            st_ref[h] = dec[:, sl] * st + _dot_tn(vb[:, sl], kb[:, sl])
            if post:
                o_scr[rs, sl] = o
            else:
                o_ref[rs, sl] = o
    if post:
        nrm = nrm_ref[...]
        zg = _silu(z_ref[...])
        for h in range(HG_HEADS):
            sl = slice(h * HG_DK, (h + 1) * HG_DK)
            o = of_ref[:, sl] + o_scr[:, sl]
            ms = jnp.mean(o * o, axis=-1, keepdims=True)
            o_ref[:, sl] = (o * lax.rsqrt(ms + EPS)) * nrm[:, sl] * zg[:, sl]


def _hgrn_call(ua, of, st_in, lb, nrm, d, reverse, name):
    tp = ua.shape[0]
    tb = min(256, tp)
    nblk = tp // tb
    post = of is not None
    tix = (lambda i: nblk - 1 - i) if reverse else (lambda i: i)
    blk = lambda k: pl.BlockSpec((tb, BR_W), lambda b, i: (tix(i), b * 5 + k))
    oblk = pl.BlockSpec((tb, BR_W), lambda b, i: (tix(i), b))
    c2 = lambda b, i: (0, 0)
    stspec = pl.BlockSpec((None, HG_HEADS, HG_DK, HG_DK), lambda b, i: (b, 0, 0, 0))
    in_specs = [blk(0), blk(1), blk(2 + d)]
    args = [ua, ua, ua]
    scratch = []
    if post:
        in_specs += [blk(4), oblk, pl.BlockSpec((1, BR_W), c2), pl.BlockSpec((1, BR_W), c2)]
        args += [ua, of, lb, nrm]
        scratch = [pltpu.VMEM((tb, BR_W), F32)]
    else:
        in_specs += [pl.BlockSpec((1, BR_W), c2)]
        args += [lb]
    in_specs.append(stspec)
    args.append(st_in)
    return pl.pallas_call(
        functools.partial(_hgrn_kernel, reverse=reverse, post=post, tb=tb),
        out_shape=[jax.ShapeDtypeStruct((tp, NB * BR_W), F32),
                   jax.ShapeDtypeStruct(st_in.shape, F32)],
        grid=(NB, nblk),
        in_specs=in_specs,
        out_specs=[oblk, stspec],
        scratch_shapes=scratch,
        compiler_params=_cp(2),
        name=name,
    )(*args)


def _ssd_kernel(*refs, reverse, post, tb, period):
    if post:
        (x_ref, dt_ref, z_ref, of_ref, cw_ref, cb_ref, dtb_ref, a_ref, e_ref, dsk_ref, nrm_ref,
         st_in_ref, o_ref, st_ref, o_scr) = refs
    else:
        (x_ref, dt_ref, cw_ref, cb_ref, dtb_ref, a_ref, e_ref, st_in_ref, o_ref, st_ref) = refs

    @pl.when(pl.program_id(1) == 0)
    def _():
        st_ref[...] = st_in_ref[...]

    gn = M2_GROUPS * M2_STATE
    x = x_ref[...]
    tpos = lax.broadcasted_iota(jnp.int32, (tb, 1), 0) % period
    xc = cb_ref[...] + cw_ref[1:2, :] * x
    xc = xc + cw_ref[0:1, :] * jnp.where(tpos >= 1, pltpu.roll(x, 1, 0), 0.0)
    xc = xc + cw_ref[2:3, :] * jnp.where(tpos <= period - 2, pltpu.roll(x, tb - 1, 0), 0.0)
    xc = xc + cw_ref[3:4, :] * jnp.where(tpos <= period - 3, pltpu.roll(x, tb - 2, 0), 0.0)
    xs = _silu(xc)
    dt = _softplus(dt_ref[...] + dtb_ref[...])
    da = dt * a_ref[...]

    ri = lax.broadcasted_iota(jnp.int32, (CHUNK, BR_W), 0)
    cj = lax.broadcasted_iota(jnp.int32, (CHUNK, BR_W), 1) % CHUNK
    mask = (cj >= ri) if reverse else (cj <= ri)
    teye = (cj == ri).astype(F32)
    tri = _tri_mask(CHUNK, reverse).astype(F32)
    r8 = lax.broadcasted_iota(jnp.int32, (BR_W, BR_W), 0) // M2_HEADDIM
    c8 = lax.broadcasted_iota(jnp.int32, (BR_W, BR_W), 1) // M2_HEADDIM
    bdmask = r8 == c8
    hpg = M2_HEADS // M2_GROUPS
    g_row = lax.broadcasted_iota(jnp.int32, (BR_W, gn), 0) // (hpg * CHUNK)
    g_col = lax.broadcasted_iota(jnp.int32, (BR_W, gn), 1) // M2_STATE
    bmask = g_row == g_col
    s_row = lax.broadcasted_iota(jnp.int32, (gn, BR_W), 0) // M2_STATE
    s_col = lax.broadcasted_iota(jnp.int32, (gn, BR_W), 1) // (hpg * M2_HEADDIM)
    smask = s_row == s_col

    nchunk = tb // CHUNK
    order = range(nchunk - 1, -1, -1) if reverse else range(nchunk)
    for c in order:
        rs = slice(c * CHUNK, (c + 1) * CHUNK)
        xh = xs[rs, 0:BR_W]
        bm = xs[rs, BR_W:BR_W + gn]
        cm = xs[rs, BR_W + gn:BR_W + 2 * gn].astype(BF16)
        da_e = _dot_hi(da[rs, :], e_ref[...])
        dt_e = _dot_hi(dt[rs, :], e_ref[...])
        cum = _dot_hi(tri, da_e)
        cum_end = cum[0:1, :] if reverse else cum[CHUNK - 1:CHUNK, :]
        cum_row = jnp.sum(cum * teye, axis=0, keepdims=True)
        dt_row = jnp.sum(dt_e * teye, axis=0, keepdims=True)
        decay = jnp.exp(jnp.where(mask, cum - cum_row, -jnp.inf))
        bm8 = jnp.where(bmask, jnp.concatenate([bm] * M2_HEADS, axis=0), 0.0).astype(BF16)
        scores = _dot_nt(cm, bm8)
        w = (scores * decay * dt_row).astype(BF16)
        xbd = jnp.where(bdmask, jnp.concatenate([xh] * M2_HEADS, axis=0), 0.0).astype(BF16)
        st = st_ref[...]
        y = _dot(w, xbd) + _dot(cm, st.astype(BF16)) * jnp.exp(cum)
        wx = (xh * (jnp.exp(cum_end - cum) * dt_e)).astype(BF16)
        st_ref[...] = jnp.exp(cum_end) * st + jnp.where(smask, _dot_tn(bm.astype(BF16), wx), 0.0)
        if post:
            o_scr[rs, :] = y
        else:
            o_ref[rs, :] = y
    if post:
        y = of_ref[...] + o_scr[...] + dsk_ref[...] * xs[:, 0:BR_W]
        yz = y * _silu(z_ref[...])
        ms = jnp.mean(yz * yz, axis=-1, keepdims=True)
        o_ref[...] = (yz * lax.rsqrt(ms + EPS)) * nrm_ref[...]


def _ssd_call(ux, udt, uz, of, st_in, cw, cb, dtb, a, e, dsk, nrm, reverse, period, name):
    tp = ux.shape[0]
    tb = min(256, tp)
    if period != CHUNK:
        assert period == tb == tp, "context conv needs the whole context in one tile"
    nblk = tp // tb
    post = of is not None
    gn = M2_GROUPS * M2_STATE
    tix = (lambda i: nblk - 1 - i) if reverse else (lambda i: i)
    tile = lambda w: pl.BlockSpec((tb, w), lambda b, i: (tix(i), b))
    c2 = lambda b, i: (0, 0)
    stspec = pl.BlockSpec((None, gn, BR_W), lambda b, i: (b, 0, 0))
    in_specs = [tile(M2_XBC), tile(DT_PAD)]
    args = [ux, udt]
    scratch = []
    if post:
        in_specs += [tile(BR_W), tile(BR_W)]
        args += [uz, of]
        scratch = [pltpu.VMEM((tb, BR_W), F32)]
    in_specs += [pl.BlockSpec((CONV_W, M2_XBC), c2), pl.BlockSpec((1, M2_XBC), c2),
                 pl.BlockSpec((1, DT_PAD), c2), pl.BlockSpec((1, DT_PAD), c2),
                 pl.BlockSpec((DT_PAD, BR_W), c2)]
    args += [cw, cb, dtb, a, e]
    if post:
        in_specs += [pl.BlockSpec((1, BR_W), c2), pl.BlockSpec((1, BR_W), c2)]
        args += [dsk, nrm]
    in_specs.append(stspec)
    args.append(st_in)
    return pl.pallas_call(
        functools.partial(_ssd_kernel, reverse=reverse, post=post, tb=tb, period=period),
        out_shape=[jax.ShapeDtypeStruct((tp, NB * BR_W), F32),
                   jax.ShapeDtypeStruct(st_in.shape, F32)],
        grid=(NB, nblk),
        in_specs=in_specs,
        out_specs=[tile(BR_W), stspec],
        scratch_shapes=scratch,
        compiler_params=_cp(2),
        name=name,
    )(*args)


def _block_diag(blocks):
    n, a, b = blocks.shape
    eye = jnp.eye(n, dtype=blocks.dtype)
    return jnp.einsum('nab,nm->namb', blocks, eye).reshape(n * a, n * b)


def _s5_params(a_re, a_im, log_step, b_re, b_im, c_re, c_im):
    g, n, p = b_re.shape
    gpc = g // S5_NCH
    lam = lax.complex(a_re.astype(F32), a_im.astype(F32))
    step = jnp.exp(log_step.astype(F32))[..., None]
    a_bar = jnp.exp(lam * step)
    coef = (a_bar - 1.0) / lam

    def chunked(m):
        return jax.vmap(_block_diag)(m.reshape(S5_NCH, gpc, m.shape[1], m.shape[2]))

    b_t = lambda m: jnp.swapaxes(m.astype(F32), 1, 2)
    bm = jnp.concatenate([chunked(b_t(b_re)), chunked(b_t(b_im))], axis=-1).astype(BF16)
    abs_, cms = [], []
    for d in range(2):
        ab = jnp.concatenate([jnp.real(a_bar[d]).reshape(S5_NCH, gpc * n),
                              jnp.imag(a_bar[d]).reshape(S5_NCH, gpc * n)], axis=-1)
        abs_.append(jnp.broadcast_to(ab[:, None, :], (S5_NCH, NB, 2 * gpc * n)))
        cc = lax.complex(c_re.astype(F32), c_im.astype(F32)) * coef[d][:, None, :]
        c_t = jnp.swapaxes(cc, 1, 2)
        cms.append(jnp.concatenate([chunked(jnp.real(c_t)), chunked(-jnp.imag(c_t))],
                                   axis=1).astype(BF16))
    return bm, abs_, cms


def _lane_vec(v, d):
    out = jnp.zeros((1, DT_PAD), F32)
    return lax.dynamic_update_slice(out, v.astype(F32)[None, :], (0, M2_HEADS * d))


def _head_expand(d):
    r = jnp.arange(DT_PAD)[:, None]
    c = jnp.arange(BR_W)[None, :] // M2_HEADDIM
    return (r == c + M2_HEADS * d).astype(F32)


def kernel(x, c, ctx, c_ctx, norm_w, w_mod, b_mod, w_in, hg_lb_logits, hg_norm, s5_a_re, s5_a_im, s5_log_step, s5_b_re, s5_b_im, s5_c_re, s5_c_im, s5_d, s5_w_glu, s5_b_glu, lru_conv_w, lru_conv_b, lru_gate_w, lru_gate_b, lru_lam, m2_conv_w, m2_conv_b, m2_dt_bias, m2_a_log, m2_d, m2_norm, w_branch, w_gate, b_gate, w_out, final_norm):
    bsz, seq, dm = x.shape
    n_ctx = ctx.shape[1]
    depth = norm_w.shape[0]
    assert bsz == NB and seq % 256 == 0 and n_ctx % CHUNK == 0

    xl = jnp.transpose(x, (1, 0, 2)).reshape(seq * NB, dm)
    xc = jnp.transpose(ctx, (1, 0, 2)).reshape(n_ctx * NB, dm)
    c_all = jnp.concatenate([c, jnp.broadcast_to(c_ctx[None, :], (NB, dm))], axis=0)
    lb_all = jnp.cumsum(jax.nn.softmax(hg_lb_logits.astype(F32), axis=0), axis=0)

    o_s5 = 5 * BR_W
    o_lru = o_s5 + 2 * BR_W
    o_m2 = o_lru + 2 * BR_W
    o_dt = o_m2 + M2_XBC
    o_mz = o_dt + 2 * M2_HEADS
    gn = M2_GROUPS * M2_STATE

    for l in range(depth):
        mod = _mod_call(c_all, w_mod[l].astype(BF16), b_mod[l][None, :])
        nw = norm_w[l][None, :]
        wl = w_in[l]
        w_a = wl[:, 0:o_s5].astype(BF16)
        w_b = wl[:, o_s5:o_lru].astype(BF16)
        w_c = wl[:, o_lru:o_m2].astype(BF16)
        w_d = jnp.concatenate(
            [wl[:, o_m2:o_dt], jnp.pad(wl[:, o_dt:o_mz], ((0, 0), (0, DT_PAD - 2 * M2_HEADS))),
             wl[:, o_mz:]], axis=1).astype(BF16)
        wg = w_gate[l].astype(BF16)
        bg = b_gate[l][:, None, :]
        wb = w_branch[l].astype(BF16)
        wo = w_out[l].astype(BF16)

        bm, s5_ab, s5_cm = _s5_params(s5_a_re[l], s5_a_im[l], s5_log_step[l], s5_b_re[l], s5_b_im[l],
                                      s5_c_re[l], s5_c_im[l])
        lru_wg = [jnp.concatenate([_block_diag(lru_gate_w[l, d, 0]), _block_diag(lru_gate_w[l, d, 1])],
                                  axis=1).astype(BF16) for d in range(2)]
        lru_bg = [lru_gate_b[l, d].reshape(1, 2 * BR_W) for d in range(2)]

        parts = []
        for name, xr, row0, period in (("ctx", xc, NB, n_ctx), ("lat", xl, 0, CHUNK)):
            sh = mod[row0:row0 + NB, 0:dm]
            sc = mod[row0:row0 + NB, dm:2 * dm]
            gt = mod[row0:row0 + NB, 2 * dm:3 * dm]
            (ua,) = _proj_call(xr, nw, sc, sh, w_a, (5 * BR_W,), "proj_hgrn_" + name)
            (ub,) = _proj_call(xr, nw, sc, sh, w_b, (2 * BR_W,), "proj_s5_" + name)
            (uc,) = _proj_call(xr, nw, sc, sh, w_c, (2 * BR_W,), "proj_lru_" + name)
            udx, udt, udz = _proj_call(xr, nw, sc, sh, w_d, (M2_XBC, DT_PAD, BR_W), "proj_ssd_" + name)
            tp = xr.shape[0] // NB
            parts.append(dict(name=name, x=xr, sc=sc, sh=sh, gt=gt, period=period,
                              ua=ua.reshape(tp, NB * 5 * BR_W), ub=ub, uc=uc,
                              udx=udx.reshape(tp, NB * M2_XBC), udt=udt.reshape(tp, NB * DT_PAD),
                              udz=udz.reshape(tp, NB * BR_W)))

        ys = [dict(), dict()]
        for d, reverse in ((0, False), (1, True)):
            tag = "bwd" if reverse else "fwd"
            st_a = jnp.zeros((NB, HG_HEADS, HG_DK, HG_DK), F32)
            st_b = jnp.zeros((NB, 2 * (BR_W // S5_GROUP) * S5_STATE), F32)
            st_c = jnp.zeros((NB, BR_W), F32)
            st_d = jnp.zeros((NB, gn, BR_W), F32)
            for pi, p in enumerate(parts):
                nm = tag + "_" + p["name"]
                tp = p["x"].shape[0] // NB
                of = ys[pi] if reverse else dict(a=None, b=None, c=None, d=None)
                oa, st_a = _hgrn_call(p["ua"], of["a"], st_a, lb_all[l, d][None, :], hg_norm[l][None, :],
                                      d, reverse, "hgrn_" + nm)
                ob, st_b = _s5_call(p["ub"], of["b"], st_b, bm, s5_ab[d], s5_cm[d], s5_d[l][None, :],
                                    s5_w_glu[l].astype(BF16), s5_b_glu[l][None, :], reverse, "s5_" + nm)
                oc, st_c = _lru_call(p["uc"], of["c"], st_c, lru_conv_w[l], lru_conv_b[l][None, :],
                                     lru_wg[d], lru_bg[d], lru_lam[l, d][None, :], reverse, p["period"],
                                     "lru_" + nm)
                od, st_d = _ssd_call(p["udx"], p["udt"], p["udz"], of["d"], st_d, m2_conv_w[l],
                                     m2_conv_b[l][None, :], _lane_vec(m2_dt_bias[l, d], d),
                                     _lane_vec(-jnp.exp(m2_a_log[l, d].astype(F32)), d), _head_expand(d),
                                     jnp.repeat(m2_d[l].astype(F32), M2_HEADDIM)[None, :],
                                     m2_norm[l][None, :], reverse, p["period"], "ssd_" + nm)
                ys[pi] = dict(a=oa if not reverse else oa.reshape(tp * NB, BR_W),
                              b=ob, c=oc,
                              d=od if not reverse else od.reshape(tp * NB, BR_W))

        last = l == depth - 1
        for pi, p in enumerate(parts):
            if last and p["name"] == "ctx":
                continue
            y = ys[pi]
            out = _merge_call(p["x"], (y["a"], y["b"], y["c"], y["d"]), nw, p["sc"], p["sh"], p["gt"],
                              wg, bg, wb, wo, final_norm[None, :], last, "merge_" + p["name"])
            if p["name"] == "ctx":
                xc = out
            else:
                xl = out
    return jnp.transpose(xl.reshape(seq, NB, dm), (1, 0, 2))
```

```python
import functools
import math

import jax
import jax.numpy as jnp
from jax import lax
from jax.experimental import pallas as pl
from jax.experimental.pallas import tpu as pltpu

F32 = jnp.float32
BF16 = jnp.bfloat16

NB = 8
EPS = 1e-6
CHUNK = 64
TILE = CHUNK * NB
CONV_W = 4
BR_W = 512
HG_HEADS = 4
HG_DK = BR_W // HG_HEADS
S5_GROUP = 16
S5_STATE = 64
S5_NCH = 4
LRU_C = 8.0
M2_HEADDIM = 64
M2_HEADS = BR_W // M2_HEADDIM
M2_GROUPS = 2
M2_STATE = 64
M2_XBC = BR_W + 2 * M2_GROUPS * M2_STATE
DT_PAD = 128
VMEM_LIMIT = 56 * 1024 * 1024


def _cp(n_axes):
    return pltpu.CompilerParams(dimension_semantics=("arbitrary",) * n_axes,
                                vmem_limit_bytes=VMEM_LIMIT)


def _const_spec(shape):
    nd = len(shape)
    return pl.BlockSpec(shape, lambda *_: (0,) * nd, pipeline_mode=pl.Buffered(1))


def _dot(a, b):
    return jnp.dot(a, b, preferred_element_type=F32)


def _dot_nt(a, b):
    return lax.dot_general(a, b, (((1,), (1,)), ((), ())), preferred_element_type=F32)


def _dot_tn(a, b):
    return lax.dot_general(a, b, (((0,), (0,)), ((), ())), preferred_element_type=F32)


def _split3(x):
    hi = x.astype(BF16)
    r1 = x - hi.astype(F32)
    mid = r1.astype(BF16)
    lo = (r1 - mid.astype(F32)).astype(BF16)
    return hi, mid, lo


def _sum3(y, n, axis):
    if axis == 1:
        return (y[:, 0:n] + y[:, n:2 * n]) + y[:, 2 * n:3 * n]
    return (y[0:n] + y[n:2 * n]) + y[2 * n:3 * n]


def _chunk_cumsum(x, reverse):
    n = x.shape[0]
    i = lax.broadcasted_iota(jnp.int32, (n, n), 0)
    j = lax.broadcasted_iota(jnp.int32, (n, n), 1)
    tri = ((i // CHUNK == j // CHUNK) & ((j >= i) if reverse else (j <= i))).astype(BF16)
    y = _dot(tri, jnp.concatenate(_split3(x), axis=1))
    return _sum3(y, x.shape[1], 1)


def _sigmoid(x):
    return 1.0 / (1.0 + jnp.exp(-x))


def _silu(x):
    return x * _sigmoid(x)


def _softplus(x):
    return jnp.maximum(x, 0.0) + jnp.log1p(jnp.exp(-jnp.abs(x)))


def _adaln(x, nw, sc, sh):
    tm, d = x.shape
    ms = jnp.mean(x * x, axis=-1, keepdims=True)
    xn = (x * lax.rsqrt(ms + EPS)) * nw
    xn3 = xn.reshape(tm // NB, NB, d)
    h = xn3 * (1.0 + sc)[None] + sh[None]
    return h.reshape(tm, d)


def _tile_perm():
    i = jnp.arange(TILE)
    src = (i % CHUNK) * NB + i // CHUNK
    return (src[:, None] == jnp.arange(TILE)[None, :]).astype(BF16)


def _mod_kernel(c_ref, w_ref, b_ref, o_ref):
    s = _silu(c_ref[...]).astype(BF16)
    o_ref[...] = _dot(s, w_ref[...]) + b_ref[...]


def _mod_call(c_all, w, b):
    n, d = c_all.shape
    m = w.shape[1]
    return pl.pallas_call(
        _mod_kernel,
        out_shape=jax.ShapeDtypeStruct((n, m), F32),
        grid=(1,),
        in_specs=[pl.BlockSpec((n, d), lambda i: (0, 0)),
                  pl.BlockSpec((d, m), lambda i: (0, 0)),
                  pl.BlockSpec((1, m), lambda i: (0, 0))],
        out_specs=pl.BlockSpec((n, m), lambda i: (0, 0)),
        compiler_params=_cp(1),
        name="mod",
    )(c_all, w, b)


def _proj_tm_kernel(x_ref, nw_ref, sc_ref, sh_ref, w_ref, *o_refs, widths):
    hb = _adaln(x_ref[...], nw_ref[...], sc_ref[...], sh_ref[...]).astype(BF16)
    off = 0
    for o_ref, wd in zip(o_refs, widths):
        o_ref[...] = _dot(hb, w_ref[:, off:off + wd])
        off += wd


def _proj_bm_kernel(x_ref, nw_ref, sc_ref, sh_ref, p_ref, w_ref, *o_refs, widths):
    hb = _adaln(x_ref[...], nw_ref[...], sc_ref[...], sh_ref[...]).astype(BF16)
    hb = _dot(p_ref[...], hb).astype(BF16)
    off = 0
    for o_ref, wd in zip(o_refs, widths):
        for c0 in range(0, wd, BR_W):
            cw = min(BR_W, wd - c0)
            u = _dot(hb, w_ref[:, off + c0:off + c0 + cw])
            for b in range(NB):
                o_ref[:, b * wd + c0:b * wd + c0 + cw] = u[b * CHUNK:(b + 1) * CHUNK, :]
        off += wd


def _proj_call(xr, nw, sc, sh, w, widths, perm, name):
    r, d = xr.shape
    n = w.shape[1]
    row = lambda i: (i, 0)
    in_specs = [pl.BlockSpec((TILE, d), row), _const_spec((1, d)), _const_spec((NB, d)),
                _const_spec((NB, d))]
    args = [xr, nw, sc, sh]
    if perm is None:
        kern = _proj_tm_kernel
        out_shape = [jax.ShapeDtypeStruct((r, wd), F32) for wd in widths]
        out_specs = [pl.BlockSpec((TILE, wd), row) for wd in widths]
    else:
        kern = _proj_bm_kernel
        in_specs.append(_const_spec((TILE, TILE)))
        args.append(perm)
        out_shape = [jax.ShapeDtypeStruct((r // NB, NB * wd), F32) for wd in widths]
        out_specs = [pl.BlockSpec((CHUNK, NB * wd), row) for wd in widths]
    in_specs.append(_const_spec((d, n)))
    args.append(w)
    return pl.pallas_call(
        functools.partial(kern, widths=widths),
        out_shape=out_shape,
        grid=(r // TILE,),
        in_specs=in_specs,
        out_specs=out_specs,
        compiler_params=_cp(1),
        name=name,
    )(*args)


def _merge_kernel(x_ref, ya_ref, yb_ref, yc_ref, yd_ref, nw_ref, sc_ref, sh_ref, gt_ref, pt_ref,
                  wg_ref, bg_ref, wb_ref, wo_ref, fn_ref, o_ref, *, final):
    x = x_ref[...]
    tm, d = x.shape
    hb = _adaln(x, nw_ref[...], sc_ref[...], sh_ref[...]).astype(BF16)

    def to_tm(y_ref):
        y_bm = jnp.concatenate([y_ref[:, b * BR_W:(b + 1) * BR_W] for b in range(NB)], axis=0)
        return _dot(pt_ref[...], y_bm.astype(BF16)).astype(BF16)

    ys = (to_tm(ya_ref), yb_ref[...].astype(BF16), yc_ref[...].astype(BF16), to_tm(yd_ref))
    m = None
    for k, y in enumerate(ys):
        g = _sigmoid(_dot(hb, wg_ref[k]) + bg_ref[k])
        p = _dot(y, wb_ref[k])
        m = g * p if m is None else m + g * p
    upd = _dot(m.astype(BF16), wo_ref[...])
    out = x + (upd.reshape(tm // NB, NB, d) * gt_ref[...][None]).reshape(tm, d)
    if final:
        ms = jnp.mean(out * out, axis=-1, keepdims=True)
        out = (out * lax.rsqrt(ms + EPS)) * fn_ref[...]
    o_ref[...] = out


def _merge_call(xr, ys, nw, sc, sh, gt, perm_t, wg, bg, wb, wo, fn, final, name):
    r, d = xr.shape
    row = lambda i: (i, 0)
    tm_spec = pl.BlockSpec((TILE, BR_W), row)
    bm_spec = pl.BlockSpec((CHUNK, NB * BR_W), row)
    return pl.pallas_call(
        functools.partial(_merge_kernel, final=final),
        out_shape=jax.ShapeDtypeStruct((r, d), F32),
        grid=(r // TILE,),
        in_specs=[pl.BlockSpec((TILE, d), row), bm_spec, tm_spec, tm_spec, bm_spec,
                  _const_spec((1, d)), _const_spec((NB, d)), _const_spec((NB, d)),
                  _const_spec((NB, d)), _const_spec((TILE, TILE)),
                  _const_spec(wg.shape), _const_spec(bg.shape), _const_spec(wb.shape),
                  _const_spec(wo.shape), _const_spec((1, d))],
        out_specs=pl.BlockSpec((TILE, d), row),
        compiler_params=_cp(1),
        name=name,
    )(xr, *ys, nw, sc, sh, gt, perm_t, wg, bg, wb, wo, fn)


def _conv_tap_tm(x_ref, j, off, period, tt, width):
    s0 = CHUNK * j + off
    plo = (CHUNK * j // period) * period
    phi = min(plo + period, tt)
    lo, hi = max(s0, plo), min(s0 + CHUNK, phi)
    parts = []
    if lo - s0 > 0:
        parts.append(jnp.zeros(((lo - s0) * NB, width), F32))
    parts.append(x_ref[lo * NB:hi * NB, 0:width])
    if s0 + CHUNK - hi > 0:
        parts.append(jnp.zeros(((s0 + CHUNK - hi) * NB, width), F32))
    return parts[0] if len(parts) == 1 else jnp.concatenate(parts, axis=0)


def _lru_kernel(*refs, reverse, post, period, tt):
    if post:
        (u_ref, of_ref, st_in_ref, cw_ref, cb_ref, wg_ref, bg_ref, lam_ref,
         o_ref, st_ref, a_scr, b_scr) = refs
    else:
        (u_ref, st_in_ref, cw_ref, cb_ref, wg_ref, bg_ref, lam_ref,
         o_ref, st_ref, a_scr, b_scr) = refs
        of_ref = None

    @pl.when(pl.program_id(0) == 0)
    def _():
        st_ref[...] = st_in_ref[...]

    nsub = tt // CHUNK
    sp = _softplus(-lam_ref[...])
    order = range(nsub - 1, -1, -1) if reverse else range(nsub)
    for j in order:
        xc = cb_ref[...] + cw_ref[1:2, :] * _conv_tap_tm(u_ref, j, 0, period, tt, BR_W)
        for k in (0, 2, 3):
            xc = xc + cw_ref[k:k + 1, :] * _conv_tap_tm(u_ref, j, k - 1, period, tt, BR_W)
        gates = _sigmoid(_dot(xc.astype(BF16), wg_ref[...]) + bg_ref[...])
        r = gates[:, 0:BR_W]
        ig = gates[:, BR_W:2 * BR_W]
        log_a = (-LRU_C) * r * sp
        a = jnp.exp(log_a)
        a_scr[...] = a
        b_scr[...] = jnp.sqrt(-jnp.tanh(log_a) * (a * a + 1.0)) * (ig * xc)

        def body(i, h):
            t = (CHUNK - 1 - i) if reverse else i
            r0 = pl.multiple_of(t * NB, NB)
            h = a_scr[pl.ds(r0, NB), :] * h + b_scr[pl.ds(r0, NB), :]
            b_scr[pl.ds(r0, NB), :] = h
            return h

        st_ref[...] = lax.fori_loop(0, CHUNK, body, st_ref[...], unroll=8)
        rs = slice(j * TILE, (j + 1) * TILE)
        if post:
            z = u_ref[rs, BR_W:2 * BR_W]
            o_ref[rs, :] = (of_ref[rs, :] + b_scr[...]) * _silu(z)
        else:
            o_ref[rs, :] = b_scr[...]


def _lru_call(u, of, st_in, cw, cb, wg, bg, lam, reverse, period, name):
    r = u.shape[0]
    tp = r // NB
    tt = tp if period != CHUNK else CHUNK
    nblk = tp // tt
    rows = tt * NB
    post = of is not None
    tmap = (lambda i: (nblk - 1 - i, 0)) if reverse else (lambda i: (i, 0))
    c2 = lambda i: (0, 0)
    in_specs = [pl.BlockSpec((rows, 2 * BR_W), tmap)]
    args = [u]
    if post:
        in_specs.append(pl.BlockSpec((rows, BR_W), tmap))
        args.append(of)
    in_specs += [pl.BlockSpec((NB, BR_W), c2), pl.BlockSpec((CONV_W, BR_W), c2),
                 pl.BlockSpec((1, BR_W), c2), pl.BlockSpec((BR_W, 2 * BR_W), c2),
                 pl.BlockSpec((1, 2 * BR_W), c2), pl.BlockSpec((1, BR_W), c2)]
    args += [st_in, cw, cb, wg, bg, lam]
    return pl.pallas_call(
        functools.partial(_lru_kernel, reverse=reverse, post=post, period=period, tt=tt),
        out_shape=[jax.ShapeDtypeStruct((r, BR_W), F32),
                   jax.ShapeDtypeStruct((NB, BR_W), F32)],
        grid=(nblk,),
        in_specs=in_specs,
        out_specs=[pl.BlockSpec((rows, BR_W), tmap), pl.BlockSpec((NB, BR_W), c2)],
        scratch_shapes=[pltpu.VMEM((TILE, BR_W), F32), pltpu.VMEM((TILE, BR_W), F32)],
        compiler_params=_cp(1),
        name=name,
    )(*args)


def _gelu_tanh(x):
    return 0.5 * x * (1.0 + jnp.tanh(math.sqrt(2.0 / math.pi) * (x + 0.044715 * (x * x * x))))


def _s5_kernel(*refs, reverse, post):
    if post:
        (u_ref, of_ref, st_in_ref, bm_ref, ab_ref, cm_ref, dsk_ref, wglu_ref, bglu_ref,
         o_ref, st_ref, s_scr, y_scr) = refs
    else:
        (u_ref, st_in_ref, bm_ref, ab_ref, cm_ref, o_ref, st_ref, s_scr) = refs

    @pl.when(pl.program_id(0) == 0)
    def _():
        st_ref[...] = st_in_ref[...]

    half = (BR_W // S5_GROUP) * S5_STATE // S5_NCH
    cin = BR_W // S5_NCH
    for c in range(S5_NCH):
        ub = u_ref[:, c * cin:(c + 1) * cin].astype(BF16)
        s_scr[...] = _dot(ub, bm_ref[c])
        a_re = ab_ref[c, :, 0:half]
        a_im = ab_ref[c, :, half:2 * half]
        c0 = c * 2 * half

        def body(i, carry):
            s_re, s_im = carry
            t = (CHUNK - 1 - i) if reverse else i
            r0 = pl.multiple_of(t * NB, NB)
            n_re = a_re * s_re - a_im * s_im + s_scr[pl.ds(r0, NB), 0:half]
            n_im = a_re * s_im + a_im * s_re + s_scr[pl.ds(r0, NB), half:2 * half]
            s_scr[pl.ds(r0, NB), 0:half] = n_re
            s_scr[pl.ds(r0, NB), half:2 * half] = n_im
            return n_re, n_im

        s_re, s_im = lax.fori_loop(
            0, CHUNK, body,
            (st_ref[:, c0:c0 + half], st_ref[:, c0 + half:c0 + 2 * half]), unroll=4)
        st_ref[:, c0:c0 + half] = s_re
        st_ref[:, c0 + half:c0 + 2 * half] = s_im
        yc = _dot(s_scr[...].astype(BF16), cm_ref[c])
        if post:
            y_scr[:, c * cin:(c + 1) * cin] = yc
        else:
            o_ref[:, c * cin:(c + 1) * cin] = yc
    if post:
        u = u_ref[:, 0:BR_W]
        z = u_ref[:, BR_W:2 * BR_W]
        y = of_ref[...] + y_scr[...] + dsk_ref[...] * u
        g = _gelu_tanh(y)
        gl = _dot(g.astype(BF16), wglu_ref[...]) + bglu_ref[...]
        o_ref[...] = g * _sigmoid(gl) * _silu(z)


def _s5_call(u, of, st_in, bm, ab, cm, dsk, wglu, bglu, reverse, name):
    r = u.shape[0]
    nblk = r // TILE
    post = of is not None
    nst = st_in.shape[1]
    tmap = (lambda i: (nblk - 1 - i, 0)) if reverse else (lambda i: (i, 0))
    c2 = lambda i: (0, 0)
    c3 = lambda i: (0, 0, 0)
    in_specs = [pl.BlockSpec((TILE, 2 * BR_W), tmap)]
    args = [u]
    if post:
        in_specs.append(pl.BlockSpec((TILE, BR_W), tmap))
        args.append(of)
    in_specs += [pl.BlockSpec((NB, nst), c2), pl.BlockSpec(bm.shape, c3),
                 pl.BlockSpec(ab.shape, c3), pl.BlockSpec(cm.shape, c3)]
    args += [st_in, bm, ab, cm]
    scratch = [pltpu.VMEM((TILE, nst // S5_NCH), F32)]
    if post:
        in_specs += [pl.BlockSpec((1, BR_W), c2), pl.BlockSpec((BR_W, BR_W), c2),
                     pl.BlockSpec((1, BR_W), c2)]
        args += [dsk, wglu, bglu]
        scratch.append(pltpu.VMEM((TILE, BR_W), F32))
    return pl.pallas_call(
        functools.partial(_s5_kernel, reverse=reverse, post=post),
        out_shape=[jax.ShapeDtypeStruct((r, BR_W), F32),
                   jax.ShapeDtypeStruct((NB, nst), F32)],
        grid=(nblk,),
        in_specs=in_specs,
        out_specs=[pl.BlockSpec((TILE, BR_W), tmap), pl.BlockSpec((NB, nst), c2)],
        scratch_shapes=scratch,
        compiler_params=_cp(1),
        name=name,
    )(*args)


def _tri_mask(n, reverse):
    i = lax.broadcasted_iota(jnp.int32, (n, n), 0)
    j = lax.broadcasted_iota(jnp.int32, (n, n), 1)
    return (j >= i) if reverse else (j <= i)


def _hgrn_kernel(*refs, reverse, post, tb):
    if post:
        (q_ref, v_ref, f_ref, z_ref, of_ref, lb_ref, nrm_ref, st_in_ref,
         o_ref, st_ref, o_scr) = refs
    else:
        (q_ref, v_ref, f_ref, lb_ref, st_in_ref, o_ref, st_ref) = refs

    @pl.when(pl.program_id(1) == 0)
    def _():
        st_ref[...] = st_in_ref[...]

    mask = _tri_mask(CHUNK, reverse)
    lb = lb_ref[...]
    f_all = lb + (1.0 - lb) * _sigmoid(f_ref[...])
    b_all = _chunk_cumsum(jnp.log(f_all), reverse)
    nchunk = tb // CHUNK
    order = range(nchunk - 1, -1, -1) if reverse else range(nchunk)
    for c in order:
        rs = slice(c * CHUNK, (c + 1) * CHUNK)
        k = 1.0 - f_all[rs, :]
        qs = _silu(q_ref[rs, :])
        b = b_all[rs, :]
        b_end = b[0:1, :] if reverse else b[CHUNK - 1:CHUNK, :]
        mid = 0.5 * b_end
        qt = (qs * jnp.exp(b - mid)).astype(BF16)
        kt = (k * jnp.exp(mid - b)).astype(BF16)
        qb = (qs * jnp.exp(b)).astype(BF16)
        kb = (k * jnp.exp(b_end - b)).astype(BF16)
        dec = jnp.exp(b_end)
        vb = v_ref[rs, :].astype(BF16)
        for h in range(HG_HEADS):
            sl = slice(h * HG_DK, (h + 1) * HG_DK)
            att = jnp.where(mask, _dot_nt(qt[:, sl], kt[:, sl]), 0.0)
            st = st_ref[h]
            o = _dot(att.astype(BF16), vb[:, sl]) + _dot_nt(qb[:, sl], st.astype(BF16))
            st_ref[h] = dec[:, sl] * st + _dot_tn(vb[:, sl], kb[:, sl])
            if post:
                o_scr[rs, sl] = o
            else:
                o_ref[rs, sl] = o
    if post:
        nrm = nrm_ref[...]
        zg = _silu(z_ref[...])
        for h in range(HG_HEADS):
            sl = slice(h * HG_DK, (h + 1) * HG_DK)
            o = of_ref[:, sl] + o_scr[:, sl]
            ms = jnp.mean(o * o, axis=-1, keepdims=True)
            o_ref[:, sl] = (o * lax.rsqrt(ms + EPS)) * nrm[:, sl] * zg[:, sl]


def _hgrn_call(ua, of, st_in, lb, nrm, d, reverse, name):
    tp = ua.shape[0]
    tb = min(256, tp)
    nblk = tp // tb
    post = of is not None
    tix = (lambda i: nblk - 1 - i) if reverse else (lambda i: i)
    blk = lambda k: pl.BlockSpec((tb, BR_W), lambda b, i: (tix(i), b * 5 + k))
    oblk = pl.BlockSpec((tb, BR_W), lambda b, i: (tix(i), b))
    c2 = lambda b, i: (0, 0)
    stspec = pl.BlockSpec((None, HG_HEADS, HG_DK, HG_DK), lambda b, i: (b, 0, 0, 0))
    in_specs = [blk(0), blk(1), blk(2 + d)]
    args = [ua, ua, ua]
    scratch = []
    if post:
        in_specs += [blk(4), oblk, pl.BlockSpec((1, BR_W), c2), pl.BlockSpec((1, BR_W), c2)]
        args += [ua, of, lb, nrm]
        scratch = [pltpu.VMEM((tb, BR_W), F32)]
    else:
        in_specs += [pl.BlockSpec((1, BR_W), c2)]
        args += [lb]
    in_specs.append(stspec)
    args.append(st_in)
    return pl.pallas_call(
        functools.partial(_hgrn_kernel, reverse=reverse, post=post, tb=tb),
        out_shape=[jax.ShapeDtypeStruct((tp, NB * BR_W), F32),
                   jax.ShapeDtypeStruct(st_in.shape, F32)],
        grid=(NB, nblk),
        in_specs=in_specs,
        out_specs=[oblk, stspec],
        scratch_shapes=scratch,
        compiler_params=_cp(2),
        name=name,
    )(*args)


def _ssd_kernel(*refs, reverse, post, tb, period):
    if post:
        (x_ref, dt_ref, z_ref, of_ref, cw_ref, cb_ref, dtb_ref, a_ref, e_ref, dsk_ref, nrm_ref,
         st_in_ref, o_ref, st_ref, o_scr) = refs
    else:
        (x_ref, dt_ref, cw_ref, cb_ref, dtb_ref, a_ref, e_ref, st_in_ref, o_ref, st_ref) = refs

    @pl.when(pl.program_id(1) == 0)
    def _():
        st_ref[...] = st_in_ref[...]

    gn = M2_GROUPS * M2_STATE
    x = x_ref[...]
    tpos = lax.broadcasted_iota(jnp.int32, (tb, 1), 0) % period
    xc = cb_ref[...] + cw_ref[1:2, :] * x
    xc = xc + cw_ref[0:1, :] * jnp.where(tpos >= 1, pltpu.roll(x, 1, 0), 0.0)
    xc = xc + cw_ref[2:3, :] * jnp.where(tpos <= period - 2, pltpu.roll(x, tb - 1, 0), 0.0)
    xc = xc + cw_ref[3:4, :] * jnp.where(tpos <= period - 3, pltpu.roll(x, tb - 2, 0), 0.0)
    xs = _silu(xc)
    dt = _softplus(dt_ref[...] + dtb_ref[...])
    cum_n = _chunk_cumsum(dt * a_ref[...], reverse)
    ex = _dot(jnp.concatenate(_split3(cum_n) + _split3(dt), axis=0), e_ref[...])
    cum_all = _sum3(ex[0:3 * tb], tb, 0)
    dt_all = _sum3(ex[3 * tb:6 * tb], tb, 0)

    ri = lax.broadcasted_iota(jnp.int32, (CHUNK, BR_W), 0)
    cj = lax.broadcasted_iota(jnp.int32, (CHUNK, BR_W), 1) % CHUNK
    mask = (cj >= ri) if reverse else (cj <= ri)
    teye = (cj == ri).astype(F32)
    r8 = lax.broadcasted_iota(jnp.int32, (BR_W, BR_W), 0) // M2_HEADDIM
    c8 = lax.broadcasted_iota(jnp.int32, (BR_W, BR_W), 1) // M2_HEADDIM
    bdmask = r8 == c8
    hpg = M2_HEADS // M2_GROUPS
    g_row = lax.broadcasted_iota(jnp.int32, (BR_W, gn), 0) // (hpg * CHUNK)
    g_col = lax.broadcasted_iota(jnp.int32, (BR_W, gn), 1) // M2_STATE
    bmask = g_row == g_col
    s_row = lax.broadcasted_iota(jnp.int32, (gn, BR_W), 0) // M2_STATE
    s_col = lax.broadcasted_iota(jnp.int32, (gn, BR_W), 1) // (hpg * M2_HEADDIM)
    smask = s_row == s_col

    nchunk = tb // CHUNK
    order = range(nchunk - 1, -1, -1) if reverse else range(nchunk)
    for c in order:
        rs = slice(c * CHUNK, (c + 1) * CHUNK)
        xh = xs[rs, 0:BR_W]
        bm = xs[rs, BR_W:BR_W + gn]
        cm = xs[rs, BR_W + gn:BR_W + 2 * gn].astype(BF16)
        cum = cum_all[rs, :]
        dt_e = dt_all[rs, :]
        cum_end = cum[0:1, :] if reverse else cum[CHUNK - 1:CHUNK, :]
        cum_row = jnp.sum(cum * teye, axis=0, keepdims=True)
        dt_row = jnp.sum(dt_e * teye, axis=0, keepdims=True)
        decay = jnp.exp(jnp.where(mask, cum - cum_row, -jnp.inf))
        bm8 = jnp.where(bmask, jnp.concatenate([bm] * M2_HEADS, axis=0), 0.0).astype(BF16)
        scores = _dot_nt(cm, bm8)
        w = (scores * decay * dt_row).astype(BF16)
        xbd = jnp.where(bdmask, jnp.concatenate([xh] * M2_HEADS, axis=0), 0.0).astype(BF16)
        st = st_ref[...]
        y = _dot(w, xbd) + _dot(cm, st.astype(BF16)) * jnp.exp(cum)
        wx = (xh * (jnp.exp(cum_end - cum) * dt_e)).astype(BF16)
        st_ref[...] = jnp.exp(cum_end) * st + jnp.where(smask, _dot_tn(bm.astype(BF16), wx), 0.0)
        if post:
            o_scr[rs, :] = y
        else:
            o_ref[rs, :] = y
    if post:
        y = of_ref[...] + o_scr[...] + dsk_ref[...] * xs[:, 0:BR_W]
        yz = y * _silu(z_ref[...])
        ms = jnp.mean(yz * yz, axis=-1, keepdims=True)
        o_ref[...] = (yz * lax.rsqrt(ms + EPS)) * nrm_ref[...]


def _ssd_call(ux, udt, uz, of, st_in, cw, cb, dtb, a, e, dsk, nrm, reverse, period, name):
    tp = ux.shape[0]
    tb = min(256, tp)
    if period != CHUNK:
        assert period == tb == tp, "context conv needs the whole context in one tile"
    nblk = tp // tb
    post = of is not None
    gn = M2_GROUPS * M2_STATE
    tix = (lambda i: nblk - 1 - i) if reverse else (lambda i: i)
    tile = lambda w: pl.BlockSpec((tb, w), lambda b, i: (tix(i), b))
    c2 = lambda b, i: (0, 0)
    stspec = pl.BlockSpec((None, gn, BR_W), lambda b, i: (b, 0, 0))
    in_specs = [tile(M2_XBC), tile(DT_PAD)]
    args = [ux, udt]
    scratch = []
    if post:
        in_specs += [tile(BR_W), tile(BR_W)]
        args += [uz, of]
        scratch = [pltpu.VMEM((tb, BR_W), F32)]
    in_specs += [pl.BlockSpec((CONV_W, M2_XBC), c2), pl.BlockSpec((1, M2_XBC), c2),
                 pl.BlockSpec((1, DT_PAD), c2), pl.BlockSpec((1, DT_PAD), c2),
                 pl.BlockSpec((DT_PAD, BR_W), c2)]
    args += [cw, cb, dtb, a, e]
    if post:
        in_specs += [pl.BlockSpec((1, BR_W), c2), pl.BlockSpec((1, BR_W), c2)]
        args += [dsk, nrm]
    in_specs.append(stspec)
    args.append(st_in)
    return pl.pallas_call(
        functools.partial(_ssd_kernel, reverse=reverse, post=post, tb=tb, period=period),
        out_shape=[jax.ShapeDtypeStruct((tp, NB * BR_W), F32),
                   jax.ShapeDtypeStruct(st_in.shape, F32)],
        grid=(NB, nblk),
        in_specs=in_specs,
        out_specs=[tile(BR_W), stspec],
        scratch_shapes=scratch,
        compiler_params=_cp(2),
        name=name,
    )(*args)


def _block_diag(blocks):
    n, a, b = blocks.shape
    eye = jnp.eye(n, dtype=blocks.dtype)
    return jnp.einsum('nab,nm->namb', blocks, eye).reshape(n * a, n * b)


def _s5_params(a_re, a_im, log_step, b_re, b_im, c_re, c_im):
    g, n, p = b_re.shape
    gpc = g // S5_NCH
    ar, ai = a_re.astype(F32), a_im.astype(F32)
    step = jnp.exp(log_step.astype(F32))[..., None]
    mag = jnp.exp(ar * step)
    ab_re, ab_im = mag * jnp.cos(ai * step), mag * jnp.sin(ai * step)
    den = ar * ar + ai * ai
    co_re = ((ab_re - 1.0) * ar + ab_im * ai) / den
    co_im = (ab_im * ar - (ab_re - 1.0) * ai) / den

    def chunked(m):
        return jax.vmap(_block_diag)(m.reshape(S5_NCH, gpc, m.shape[1], m.shape[2]))

    b_t = lambda m: jnp.swapaxes(m.astype(F32), 1, 2)
    bm = jnp.concatenate([chunked(b_t(b_re)), chunked(b_t(b_im))], axis=-1).astype(BF16)
    cr, ci = c_re.astype(F32), c_im.astype(F32)
    abs_, cms = [], []
    for d in range(2):
        ab = jnp.concatenate([ab_re[d].reshape(S5_NCH, gpc * n), ab_im[d].reshape(S5_NCH, gpc * n)],
                             axis=-1)
        abs_.append(jnp.broadcast_to(ab[:, None, :], (S5_NCH, NB, 2 * gpc * n)))
        cc_re = cr * co_re[d][:, None, :] - ci * co_im[d][:, None, :]
        cc_im = cr * co_im[d][:, None, :] + ci * co_re[d][:, None, :]
        to_rows = lambda m: chunked(jnp.swapaxes(m, 1, 2))
        cms.append(jnp.concatenate([to_rows(cc_re), to_rows(-cc_im)], axis=1).astype(BF16))
    return bm, abs_, cms


def _lane_vec(v, d):
    out = jnp.zeros((1, DT_PAD), F32)
    return lax.dynamic_update_slice(out, v.astype(F32)[None, :], (0, M2_HEADS * d))


def _head_expand(d):
    r = jnp.arange(DT_PAD)[:, None]
    c = jnp.arange(BR_W)[None, :] // M2_HEADDIM
    return (r == c + M2_HEADS * d).astype(BF16)


def kernel(x, c, ctx, c_ctx, norm_w, w_mod, b_mod, w_in, hg_lb_logits, hg_norm, s5_a_re, s5_a_im, s5_log_step, s5_b_re, s5_b_im, s5_c_re, s5_c_im, s5_d, s5_w_glu, s5_b_glu, lru_conv_w, lru_conv_b, lru_gate_w, lru_gate_b, lru_lam, m2_conv_w, m2_conv_b, m2_dt_bias, m2_a_log, m2_d, m2_norm, w_branch, w_gate, b_gate, w_out, final_norm):
    bsz, seq, dm = x.shape
    n_ctx = ctx.shape[1]
    depth = norm_w.shape[0]
    assert bsz == NB and seq % 256 == 0 and n_ctx % CHUNK == 0

    xl = jnp.transpose(x, (1, 0, 2)).reshape(seq * NB, dm)
    xc = jnp.transpose(ctx, (1, 0, 2)).reshape(n_ctx * NB, dm)
    c_all = jnp.concatenate([c, jnp.broadcast_to(c_ctx[None, :], (NB, dm))], axis=0)
    lb_all = jnp.cumsum(jax.nn.softmax(hg_lb_logits.astype(F32), axis=0), axis=0)
    perm = _tile_perm()
    perm_t = perm.T

    o_s5 = 5 * BR_W
    o_lru = o_s5 + 2 * BR_W
    o_m2 = o_lru + 2 * BR_W
    o_dt = o_m2 + M2_XBC
    o_mz = o_dt + 2 * M2_HEADS
    gn = M2_GROUPS * M2_STATE

    for l in range(depth):
        mod = _mod_call(c_all, w_mod[l].astype(BF16), b_mod[l][None, :])
        nw = norm_w[l][None, :]
        wl = w_in[l]
        w_tm = wl[:, o_s5:o_m2].astype(BF16)
        w_bm = jnp.concatenate(
            [wl[:, 0:o_s5], wl[:, o_m2:o_dt],
             jnp.pad(wl[:, o_dt:o_mz], ((0, 0), (0, DT_PAD - 2 * M2_HEADS))), wl[:, o_mz:]],
            axis=1).astype(BF16)
        wg = w_gate[l].astype(BF16)
        bg = b_gate[l][:, None, :]
        wb = w_branch[l].astype(BF16)
        wo = w_out[l].astype(BF16)

        bm, s5_ab, s5_cm = _s5_params(s5_a_re[l], s5_a_im[l], s5_log_step[l], s5_b_re[l], s5_b_im[l],
                                      s5_c_re[l], s5_c_im[l])
        lru_wg = [jnp.concatenate([_block_diag(lru_gate_w[l, d, 0]), _block_diag(lru_gate_w[l, d, 1])],
                                  axis=1).astype(BF16) for d in range(2)]
        lru_bg = [lru_gate_b[l, d].reshape(1, 2 * BR_W) for d in range(2)]

        parts = []
        for name, xr, row0, period in (("ctx", xc, NB, n_ctx), ("lat", xl, 0, CHUNK)):
            sh = mod[row0:row0 + NB, 0:dm]
            sc = mod[row0:row0 + NB, dm:2 * dm]
            gt = mod[row0:row0 + NB, 2 * dm:3 * dm]
            ub, uc = _proj_call(xr, nw, sc, sh, w_tm, (2 * BR_W, 2 * BR_W), None, "proj_tm_" + name)
            ua, udx, udt, udz = _proj_call(xr, nw, sc, sh, w_bm, (5 * BR_W, M2_XBC, DT_PAD, BR_W), perm,
                                           "proj_bm_" + name)
            parts.append(dict(name=name, x=xr, sc=sc, sh=sh, gt=gt, period=period,
                              ua=ua, ub=ub, uc=uc, udx=udx, udt=udt, udz=udz))

        ys = [dict(), dict()]
        for d, reverse in ((0, False), (1, True)):
            tag = "bwd" if reverse else "fwd"
            st_a = jnp.zeros((NB, HG_HEADS, HG_DK, HG_DK), F32)
            st_b = jnp.zeros((NB, 2 * (BR_W // S5_GROUP) * S5_STATE), F32)
            st_c = jnp.zeros((NB, BR_W), F32)
            st_d = jnp.zeros((NB, gn, BR_W), F32)
            for pi, p in enumerate(parts):
                nm = tag + "_" + p["name"]
                of = ys[pi] if reverse else dict(a=None, b=None, c=None, d=None)
                oa, st_a = _hgrn_call(p["ua"], of["a"], st_a, lb_all[l, d][None, :], hg_norm[l][None, :],
                                      d, reverse, "hgrn_" + nm)
                ob, st_b = _s5_call(p["ub"], of["b"], st_b, bm, s5_ab[d], s5_cm[d], s5_d[l][None, :],
                                    s5_w_glu[l].astype(BF16), s5_b_glu[l][None, :], reverse, "s5_" + nm)
                oc, st_c = _lru_call(p["uc"], of["c"], st_c, lru_conv_w[l], lru_conv_b[l][None, :],
                                     lru_wg[d], lru_bg[d], lru_lam[l, d][None, :], reverse, p["period"],
                                     "lru_" + nm)
                od, st_d = _ssd_call(p["udx"], p["udt"], p["udz"], of["d"], st_d, m2_conv_w[l],
                                     m2_conv_b[l][None, :], _lane_vec(m2_dt_bias[l, d], d),
                                     _lane_vec(-jnp.exp(m2_a_log[l, d].astype(F32)), d), _head_expand(d),
                                     jnp.repeat(m2_d[l].astype(F32), M2_HEADDIM)[None, :],
                                     m2_norm[l][None, :], reverse, p["period"], "ssd_" + nm)
                ys[pi] = dict(a=oa, b=ob, c=oc, d=od)

        last = l == depth - 1
        for pi, p in enumerate(parts):
            if last and p["name"] == "ctx":
                continue
            y = ys[pi]
            out = _merge_call(p["x"], (y["a"], y["b"], y["c"], y["d"]), nw, p["sc"], p["sh"], p["gt"],
                              perm_t, wg, bg, wb, wo, final_norm[None, :], last, "merge_" + p["name"])
            if p["name"] == "ctx":
                xc = out
            else:
                xl = out
    return jnp.transpose(xl.reshape(seq, NB, dm), (1, 0, 2))
```

```python
import functools
import math

import jax
import jax.numpy as jnp
from jax import lax
from jax.experimental import pallas as pl
from jax.experimental.pallas import tpu as pltpu

F32 = jnp.float32
BF16 = jnp.bfloat16

NB = 8
EPS = 1e-6
CHUNK = 64
TILE = CHUNK * NB
HALO = 2 * NB
CONV_W = 4
BR_W = 512
HG_HEADS = 4
HG_DK = BR_W // HG_HEADS
S5_GROUP = 16
S5_STATE = 64
S5_NCH = 4
LRU_C = 8.0
M2_HEADDIM = 64
M2_HEADS = BR_W // M2_HEADDIM
M2_GROUPS = 2
M2_STATE = 64
M2_XBC = BR_W + 2 * M2_GROUPS * M2_STATE
DT_PAD = 128
BPS = 2
VMEM_LIMIT = 56 * 1024 * 1024


def _cp(n_axes):
    return pltpu.CompilerParams(dimension_semantics=("arbitrary",) * n_axes,
                                vmem_limit_bytes=VMEM_LIMIT)


def _const_spec(shape):
    nd = len(shape)
    return pl.BlockSpec(shape, lambda *_: (0,) * nd, pipeline_mode=pl.Buffered(1))


def _dot(a, b):
    return jnp.dot(a, b, preferred_element_type=F32)


def _dot_nt(a, b):
    return lax.dot_general(a, b, (((1,), (1,)), ((), ())), preferred_element_type=F32)


def _dot_tn(a, b):
    return lax.dot_general(a, b, (((0,), (0,)), ((), ())), preferred_element_type=F32)


def _split3(x):
    hi = x.astype(BF16)
    r1 = x - hi.astype(F32)
    mid = r1.astype(BF16)
    lo = (r1 - mid.astype(F32)).astype(BF16)
    return hi, mid, lo


def _sum3(y, n, axis):
    if axis == 1:
        return (y[:, 0:n] + y[:, n:2 * n]) + y[:, 2 * n:3 * n]
    return (y[0:n] + y[n:2 * n]) + y[2 * n:3 * n]


def _chunk_cumsum(x, reverse):
    n = x.shape[0]
    i = lax.broadcasted_iota(jnp.int32, (n, n), 0)
    j = lax.broadcasted_iota(jnp.int32, (n, n), 1)
    tri = ((i // CHUNK == j // CHUNK) & ((j >= i) if reverse else (j <= i))).astype(BF16)
    y = _dot(tri, jnp.concatenate(_split3(x), axis=1))
    return _sum3(y, x.shape[1], 1)


def _sigmoid(x):
    return 0.5 * jnp.tanh(0.5 * x) + 0.5


def _silu(x):
    return x * _sigmoid(x)


def _softplus(x):
    return jnp.maximum(x, 0.0) + jnp.log1p(jnp.exp(-jnp.abs(x)))


def _adaln(x, nw, sc, sh):
    tm, d = x.shape
    ms = jnp.mean(x * x, axis=-1, keepdims=True)
    xn = (x * lax.rsqrt(ms + EPS)) * nw
    xn3 = xn.reshape(tm // NB, NB, d)
    h = xn3 * (1.0 + sc)[None] + sh[None]
    return h.reshape(tm, d)


def _tile_perm():
    i = jnp.arange(TILE)
    src = (i % CHUNK) * NB + i // CHUNK
    return (src[:, None] == jnp.arange(TILE)[None, :]).astype(BF16)


def _mod_kernel(c_ref, w_ref, b_ref, o_ref):
    s = _silu(c_ref[...]).astype(BF16)
    o_ref[...] = _dot(s, w_ref[...]) + b_ref[...]


def _mod_call(c_all, w, b):
    n, d = c_all.shape
    m = w.shape[1]
    return pl.pallas_call(
        _mod_kernel,
        out_shape=jax.ShapeDtypeStruct((n, m), F32),
        grid=(1,),
        in_specs=[pl.BlockSpec((n, d), lambda i: (0, 0)),
                  pl.BlockSpec((d, m), lambda i: (0, 0)),
                  pl.BlockSpec((1, m), lambda i: (0, 0))],
        out_specs=pl.BlockSpec((n, m), lambda i: (0, 0)),
        compiler_params=_cp(1),
        name="mod",
    )(c_all, w, b)


def _conv_tm(hb, halo, w_ref, cw_ref, cb_ref, ptiles):
    u = _dot(hb, w_ref[...])
    width = u.shape[1]
    if ptiles == 1:
        up = jnp.zeros((NB, width), F32)
        un = jnp.zeros((HALO, width), F32)
    else:
        i = pl.program_id(0)
        vp = (i % ptiles != 0).astype(F32)
        vn = (i % ptiles != ptiles - 1).astype(F32)
        up = _dot(halo[0], w_ref[...])[NB:HALO, :] * vp
        un = _dot(halo[1], w_ref[...]) * vn
    ue = jnp.concatenate([up, u, un], axis=0)
    xc = cb_ref[...] + cw_ref[0:1, :] * ue[0:TILE, :]
    for k in range(1, CONV_W):
        xc = xc + cw_ref[k:k + 1, :] * ue[k * NB:k * NB + TILE, :]
    return xc


def _proj_head(refs, ptiles):
    if ptiles == 1:
        x_ref, nw_ref, sc_ref, sh_ref = refs[:4]
        rest = refs[4:]
        halo = None
    else:
        x_ref, xp_ref, xn_ref, nw_ref, sc_ref, sh_ref = refs[:6]
        rest = refs[6:]
        halo = tuple(_adaln(r[...], nw_ref[...], sc_ref[...], sh_ref[...]).astype(BF16)
                     for r in (xp_ref, xn_ref))
    hb = _adaln(x_ref[...], nw_ref[...], sc_ref[...], sh_ref[...]).astype(BF16)
    return hb, halo, rest


def _proj_tm_kernel(*refs, ptiles):
    hb, halo, rest = _proj_head(refs, ptiles)
    w_s5_ref, w_lx_ref, w_lz_ref, cw_ref, cb_ref, ub_ref, ucx_ref, ucz_ref = rest
    ub_ref[...] = _dot(hb, w_s5_ref[...])
    ucx_ref[...] = _conv_tm(hb, halo, w_lx_ref, cw_ref, cb_ref, ptiles)
    ucz_ref[...] = _dot(hb, w_lz_ref[...])


def _store_bm(o_ref, u, wd, c0):
    cw = u.shape[1]
    for b in range(NB):
        o_ref[:, b * wd + c0:b * wd + c0 + cw] = u[b * CHUNK:(b + 1) * CHUNK, :].astype(o_ref.dtype)


def _proj_bm_kernel(*refs, ptiles, widths):
    hb, halo, rest = _proj_head(refs, ptiles)
    p_ref, w_x_ref, w_ref, cw_ref, cb_ref, uxs_ref = rest[:6]
    o_refs = rest[6:]
    xs = _silu(_conv_tm(hb, halo, w_x_ref, cw_ref, cb_ref, ptiles)).astype(BF16)
    _store_bm(uxs_ref, _dot(p_ref[...], xs), M2_XBC, 0)
    hb = _dot(p_ref[...], hb).astype(BF16)
    off = 0
    for o_ref, wd in zip(o_refs, widths):
        for c0 in range(0, wd, BR_W):
            cw = min(BR_W, wd - c0)
            _store_bm(o_ref, _dot(hb, w_ref[:, off + c0:off + c0 + cw]), wd, c0)
        off += wd


def _proj_specs(xr, nw, sc, sh, ptiles):
    r, d = xr.shape
    row = lambda i: (i, 0)
    in_specs = [pl.BlockSpec((TILE, d), row)]
    args = [xr]
    if ptiles > 1:
        per = TILE // HALO
        last = r // HALO - 1
        in_specs += [pl.BlockSpec((HALO, d), lambda i: (jnp.maximum(i * per - 1, 0), 0)),
                     pl.BlockSpec((HALO, d), lambda i: (jnp.minimum((i + 1) * per, last), 0))]
        args += [xr, xr]
    in_specs += [_const_spec((1, d)), _const_spec((NB, d)), _const_spec((NB, d))]
    args += [nw, sc, sh]
    return in_specs, args


def _proj_tm_call(xr, nw, sc, sh, w_s5, w_lx, w_lz, cw, cb, ptiles, name):
    r, d = xr.shape
    row = lambda i: (i, 0)
    in_specs, args = _proj_specs(xr, nw, sc, sh, ptiles)
    for a in (w_s5, w_lx, w_lz, cw, cb):
        in_specs.append(_const_spec(a.shape))
        args.append(a)
    widths = (2 * BR_W, BR_W, BR_W)
    return pl.pallas_call(
        functools.partial(_proj_tm_kernel, ptiles=ptiles),
        out_shape=[jax.ShapeDtypeStruct((r, wd), F32) for wd in widths],
        grid=(r // TILE,),
        in_specs=in_specs,
        out_specs=[pl.BlockSpec((TILE, wd), row) for wd in widths],
        compiler_params=_cp(1),
        name=name,
    )(*args)


def _proj_bm_call(xr, nw, sc, sh, perm, w_x, w, widths, cw, cb, ptiles, name):
    r, d = xr.shape
    row = lambda i: (i, 0)
    in_specs, args = _proj_specs(xr, nw, sc, sh, ptiles)
    for a in (perm, w_x, w, cw, cb):
        in_specs.append(_const_spec(a.shape))
        args.append(a)
    shapes = [(M2_XBC, BF16)] + [(wd, F32) for wd in widths]
    return pl.pallas_call(
        functools.partial(_proj_bm_kernel, ptiles=ptiles, widths=widths),
        out_shape=[jax.ShapeDtypeStruct((r // NB, NB * wd), dt) for wd, dt in shapes],
        grid=(r // TILE,),
        in_specs=in_specs,
        out_specs=[pl.BlockSpec((CHUNK, NB * wd), row) for wd, _ in shapes],
        compiler_params=_cp(1),
        name=name,
    )(*args)


def _merge_kernel(x_ref, ya_ref, yb_ref, yc_ref, yd_ref, nw_ref, sc_ref, sh_ref, gt_ref, pt_ref,
                  wg_ref, bg_ref, wb_ref, wo_ref, fn_ref, o_ref, *, final):
    x = x_ref[...]
    tm, d = x.shape
    hb = _adaln(x, nw_ref[...], sc_ref[...], sh_ref[...]).astype(BF16)

    def to_tm(y_ref):
        y_bm = jnp.concatenate([y_ref[:, b * BR_W:(b + 1) * BR_W] for b in range(NB)], axis=0)
        return _dot(pt_ref[...], y_bm.astype(BF16)).astype(BF16)

    ys = (to_tm(ya_ref), yb_ref[...].astype(BF16), yc_ref[...].astype(BF16), to_tm(yd_ref))
    m = None
    for k, y in enumerate(ys):
        g = _sigmoid(_dot(hb, wg_ref[k]) + bg_ref[k])
        p = _dot(y, wb_ref[k])
        m = g * p if m is None else m + g * p
    upd = _dot(m.astype(BF16), wo_ref[...])
    out = x + (upd.reshape(tm // NB, NB, d) * gt_ref[...][None]).reshape(tm, d)
    if final:
        ms = jnp.mean(out * out, axis=-1, keepdims=True)
        out = (out * lax.rsqrt(ms + EPS)) * fn_ref[...]
    o_ref[...] = out


def _merge_call(xr, ys, nw, sc, sh, gt, perm_t, wg, bg, wb, wo, fn, final, name):
    r, d = xr.shape
    row = lambda i: (i, 0)
    tm_spec = pl.BlockSpec((TILE, BR_W), row)
    bm_spec = pl.BlockSpec((CHUNK, NB * BR_W), row)
    return pl.pallas_call(
        functools.partial(_merge_kernel, final=final),
        out_shape=jax.ShapeDtypeStruct((r, d), F32),
        grid=(r // TILE,),
        in_specs=[pl.BlockSpec((TILE, d), row), bm_spec, tm_spec, tm_spec, bm_spec,
                  _const_spec((1, d)), _const_spec((NB, d)), _const_spec((NB, d)),
                  _const_spec((NB, d)), _const_spec((TILE, TILE)),
                  _const_spec(wg.shape), _const_spec(bg.shape), _const_spec(wb.shape),
                  _const_spec(wo.shape), _const_spec((1, d))],
        out_specs=pl.BlockSpec((TILE, d), row),
        compiler_params=_cp(1),
        name=name,
    )(xr, *ys, nw, sc, sh, gt, perm_t, wg, bg, wb, wo, fn)


def _lru_kernel(*refs, reverse, post):
    if post:
        (x_ref, z_ref, of_ref, st_in_ref, wg_ref, bg_ref, lam_ref, o_ref, st_ref, a_scr, b_scr) = refs
    else:
        (x_ref, st_in_ref, wg_ref, bg_ref, lam_ref, o_ref, st_ref, a_scr, b_scr) = refs

    @pl.when(pl.program_id(0) == 0)
    def _():
        st_ref[...] = st_in_ref[...]

    sp = _softplus(-lam_ref[...])
    xc = x_ref[...]
    gates = _sigmoid(_dot(xc.astype(BF16), wg_ref[...]) + bg_ref[...])
    log_a = (-LRU_C) * gates[:, 0:BR_W] * sp
    a = jnp.exp(log_a)
    a_scr[...] = a
    b_scr[...] = jnp.sqrt(-jnp.tanh(log_a) * (a * a + 1.0)) * (gates[:, BR_W:2 * BR_W] * xc)
    h = st_ref[...]
    for i in range(CHUNK):
        r0 = ((CHUNK - 1 - i) if reverse else i) * NB
        h = a_scr[r0:r0 + NB, :] * h + b_scr[r0:r0 + NB, :]
        b_scr[r0:r0 + NB, :] = h
    st_ref[...] = h
    if post:
        o_ref[...] = (of_ref[...] + b_scr[...]) * _silu(z_ref[...])
    else:
        o_ref[...] = b_scr[...]


def _lru_call(xc, z, of, st_in, wg, bg, lam, reverse, name):
    r = xc.shape[0]
    nblk = r // TILE
    post = of is not None
    tmap = (lambda i: (nblk - 1 - i, 0)) if reverse else (lambda i: (i, 0))
    c2 = lambda i: (0, 0)
    tile = pl.BlockSpec((TILE, BR_W), tmap)
    in_specs = [tile]
    args = [xc]
    if post:
        in_specs += [tile, tile]
        args += [z, of]
    in_specs += [pl.BlockSpec((NB, BR_W), c2), pl.BlockSpec((BR_W, 2 * BR_W), c2),
                 pl.BlockSpec((1, 2 * BR_W), c2), pl.BlockSpec((1, BR_W), c2)]
    args += [st_in, wg, bg, lam]
    return pl.pallas_call(
        functools.partial(_lru_kernel, reverse=reverse, post=post),
        out_shape=[jax.ShapeDtypeStruct((r, BR_W), F32),
                   jax.ShapeDtypeStruct((NB, BR_W), F32)],
        grid=(nblk,),
        in_specs=in_specs,
        out_specs=[tile, pl.BlockSpec((NB, BR_W), c2)],
        scratch_shapes=[pltpu.VMEM((TILE, BR_W), F32), pltpu.VMEM((TILE, BR_W), F32)],
        compiler_params=_cp(1),
        name=name,
    )(*args)


def _gelu_tanh(x):
    return 0.5 * x * (1.0 + jnp.tanh(math.sqrt(2.0 / math.pi) * (x + 0.044715 * (x * x * x))))


def _s5_kernel(*refs, reverse, post):
    if post:
        (u_ref, of_ref, st_in_ref, bm_ref, ab_ref, cm_ref, dsk_ref, wglu_ref, bglu_ref,
         o_ref, st_ref, s_scr, y_scr) = refs
    else:
        (u_ref, st_in_ref, bm_ref, ab_ref, cm_ref, o_ref, st_ref, s_scr) = refs
        y_scr = o_ref

    @pl.when(pl.program_id(0) == 0)
    def _():
        st_ref[...] = st_in_ref[...]

    half = (BR_W // S5_GROUP) * S5_STATE // S5_NCH
    cin = BR_W // S5_NCH
    for c in range(S5_NCH):
        ub = u_ref[:, c * cin:(c + 1) * cin].astype(BF16)
        s_scr[c] = _dot(ub, bm_ref[c])
    for c in range(S5_NCH):
        a_re = ab_ref[c, :, 0:half]
        a_im = ab_ref[c, :, half:2 * half]
        c0 = c * 2 * half
        s_re = st_ref[:, c0:c0 + half]
        s_im = st_ref[:, c0 + half:c0 + 2 * half]
        for i in range(CHUNK):
            r0 = ((CHUNK - 1 - i) if reverse else i) * NB
            n_re = a_re * s_re - a_im * s_im + s_scr[c, r0:r0 + NB, 0:half]
            n_im = a_re * s_im + a_im * s_re + s_scr[c, r0:r0 + NB, half:2 * half]
            s_scr[c, r0:r0 + NB, 0:half] = n_re
            s_scr[c, r0:r0 + NB, half:2 * half] = n_im
            s_re, s_im = n_re, n_im
        st_ref[:, c0:c0 + half] = s_re
        st_ref[:, c0 + half:c0 + 2 * half] = s_im
        y_scr[:, c * cin:(c + 1) * cin] = _dot(s_scr[c].astype(BF16), cm_ref[c])
    if post:
        u = u_ref[:, 0:BR_W]
        z = u_ref[:, BR_W:2 * BR_W]
        y = of_ref[...] + y_scr[...] + dsk_ref[...] * u
        g = _gelu_tanh(y)
        gl = _dot(g.astype(BF16), wglu_ref[...]) + bglu_ref[...]
        o_ref[...] = g * _sigmoid(gl) * _silu(z)


def _s5_call(u, of, st_in, bm, ab, cm, dsk, wglu, bglu, reverse, name):
    r = u.shape[0]
    nblk = r // TILE
    post = of is not None
    nst = st_in.shape[1]
    tmap = (lambda i: (nblk - 1 - i, 0)) if reverse else (lambda i: (i, 0))
    c2 = lambda i: (0, 0)
    c3 = lambda i: (0, 0, 0)
    in_specs = [pl.BlockSpec((TILE, 2 * BR_W), tmap)]
    args = [u]
    if post:
        in_specs.append(pl.BlockSpec((TILE, BR_W), tmap))
        args.append(of)
    in_specs += [pl.BlockSpec((NB, nst), c2), pl.BlockSpec(bm.shape, c3),
                 pl.BlockSpec(ab.shape, c3), pl.BlockSpec(cm.shape, c3)]
    args += [st_in, bm, ab, cm]
    scratch = [pltpu.VMEM((S5_NCH, TILE, nst // S5_NCH), F32)]
    if post:
        in_specs += [pl.BlockSpec((1, BR_W), c2), pl.BlockSpec((BR_W, BR_W), c2),
                     pl.BlockSpec((1, BR_W), c2)]
        args += [dsk, wglu, bglu]
        scratch.append(pltpu.VMEM((TILE, BR_W), F32))
    return pl.pallas_call(
        functools.partial(_s5_kernel, reverse=reverse, post=post),
        out_shape=[jax.ShapeDtypeStruct((r, BR_W), F32),
                   jax.ShapeDtypeStruct((NB, nst), F32)],
        grid=(nblk,),
        in_specs=in_specs,
        out_specs=[pl.BlockSpec((TILE, BR_W), tmap), pl.BlockSpec((NB, nst), c2)],
        scratch_shapes=scratch,
        compiler_params=_cp(1),
        name=name,
    )(*args)


def _tri_mask(n, reverse):
    i = lax.broadcasted_iota(jnp.int32, (n, n), 0)
    j = lax.broadcasted_iota(jnp.int32, (n, n), 1)
    return (j >= i) if reverse else (j <= i)


def _hgrn_kernel(*refs, reverse, post, tb):
    if post:
        (q_ref, v_ref, f_ref, z_ref, of_ref, lb_ref, nrm_ref, st_in_ref,
         o_ref, st_ref, o_scr) = refs
    else:
        (q_ref, v_ref, f_ref, lb_ref, st_in_ref, o_ref, st_ref) = refs

    @pl.when(pl.program_id(1) == 0)
    def _():
        st_ref[...] = st_in_ref[...]

    mask = _tri_mask(CHUNK, reverse)
    lb = lb_ref[...]
    nchunk = tb // CHUNK
    order = range(nchunk - 1, -1, -1) if reverse else range(nchunk)
    for bb in range(BPS):
        co = bb * BR_W
        f_all = lb + (1.0 - lb) * _sigmoid(f_ref[:, co:co + BR_W])
        b_all = _chunk_cumsum(jnp.log(f_all), reverse)
        for c in order:
            rs = slice(c * CHUNK, (c + 1) * CHUNK)
            k = 1.0 - f_all[rs, :]
            qs = _silu(q_ref[rs, co:co + BR_W])
            b = b_all[rs, :]
            b_end = b[0:1, :] if reverse else b[CHUNK - 1:CHUNK, :]
            mid = 0.5 * b_end
            qt = (qs * jnp.exp(b - mid)).astype(BF16)
            kt = (k * jnp.exp(mid - b)).astype(BF16)
            qb = (qs * jnp.exp(b)).astype(BF16)
            kb = (k * jnp.exp(b_end - b)).astype(BF16)
            dec = jnp.exp(b_end)
            vb = v_ref[rs, co:co + BR_W].astype(BF16)
            for h in range(HG_HEADS):
                sl = slice(h * HG_DK, (h + 1) * HG_DK)
                osl = slice(co + h * HG_DK, co + (h + 1) * HG_DK)
                att = jnp.where(mask, _dot_nt(qt[:, sl], kt[:, sl]), 0.0)
                st = st_ref[bb, h]
                o = _dot(att.astype(BF16), vb[:, sl]) + _dot_nt(qb[:, sl], st.astype(BF16))
                st_ref[bb, h] = dec[:, sl] * st + _dot_tn(vb[:, sl], kb[:, sl])
                if post:
                    o_scr[rs, osl] = o
                else:
                    o_ref[rs, osl] = o
        if post:
            nrm = nrm_ref[...]
            for h in range(HG_HEADS):
                sl = slice(h * HG_DK, (h + 1) * HG_DK)
                osl = slice(co + h * HG_DK, co + (h + 1) * HG_DK)
                o = of_ref[:, osl] + o_scr[:, osl]
                ms = jnp.mean(o * o, axis=-1, keepdims=True)
                o_ref[:, osl] = (o * lax.rsqrt(ms + EPS)) * nrm[:, sl] * _silu(z_ref[:, osl])


def _hgrn_call(uq, uv, uf, uz, of, st_in, lb, nrm, reverse, name):
    tp = uq.shape[0]
    tb = min(256, tp)
    nblk = tp // tb
    post = of is not None
    tix = (lambda i: nblk - 1 - i) if reverse else (lambda i: i)
    blk = pl.BlockSpec((tb, BPS * BR_W), lambda b, i: (tix(i), b))
    c2 = lambda b, i: (0, 0)
    stspec = pl.BlockSpec((BPS,) + st_in.shape[1:], lambda b, i: (b, 0, 0, 0))
    in_specs = [blk, blk, blk]
    args = [uq, uv, uf]
    scratch = []
    if post:
        in_specs += [blk, blk, pl.BlockSpec((1, BR_W), c2), pl.BlockSpec((1, BR_W), c2)]
        args += [uz, of, lb, nrm]
        scratch = [pltpu.VMEM((tb, BPS * BR_W), F32)]
    else:
        in_specs += [pl.BlockSpec((1, BR_W), c2)]
        args += [lb]
    in_specs.append(stspec)
    args.append(st_in)
    return pl.pallas_call(
        functools.partial(_hgrn_kernel, reverse=reverse, post=post, tb=tb),
        out_shape=[jax.ShapeDtypeStruct((tp, NB * BR_W), F32),
                   jax.ShapeDtypeStruct(st_in.shape, F32)],
        grid=(NB // BPS, nblk),
        in_specs=in_specs,
        out_specs=[blk, stspec],
        scratch_shapes=scratch,
        compiler_params=_cp(2),
        name=name,
    )(*args)


def _ssd_kernel(*refs, reverse, post, tb):
    if post:
        (xs_ref, dt_ref, z_ref, of_ref, dtb_ref, a_ref, e_ref, dsk_ref, nrm_ref,
         st_in_ref, o_ref, st_ref, o_scr) = refs
    else:
        (xs_ref, dt_ref, dtb_ref, a_ref, e_ref, st_in_ref, o_ref, st_ref) = refs

    @pl.when(pl.program_id(1) == 0)
    def _():
        st_ref[...] = st_in_ref[...]

    gn = M2_GROUPS * M2_STATE
    ri = lax.broadcasted_iota(jnp.int32, (CHUNK, BR_W), 0)
    cj = lax.broadcasted_iota(jnp.int32, (CHUNK, BR_W), 1) % CHUNK
    mask = (cj >= ri) if reverse else (cj <= ri)
    teye = (cj == ri).astype(F32)
    r8 = lax.broadcasted_iota(jnp.int32, (BR_W, BR_W), 0) // M2_HEADDIM
    c8 = lax.broadcasted_iota(jnp.int32, (BR_W, BR_W), 1) // M2_HEADDIM
    bdmask = r8 == c8
    hpg = M2_HEADS // M2_GROUPS
    g_row = lax.broadcasted_iota(jnp.int32, (BR_W, gn), 0) // (hpg * CHUNK)
    g_col = lax.broadcasted_iota(jnp.int32, (BR_W, gn), 1) // M2_STATE
    bmask = g_row == g_col
    s_row = lax.broadcasted_iota(jnp.int32, (gn, BR_W), 0) // M2_STATE
    s_col = lax.broadcasted_iota(jnp.int32, (gn, BR_W), 1) // (hpg * M2_HEADDIM)
    smask = s_row == s_col
    zero = jnp.zeros((), BF16)

    nchunk = tb // CHUNK
    order = range(nchunk - 1, -1, -1) if reverse else range(nchunk)
    for bb in range(BPS):
        xo = bb * M2_XBC
        osl = slice(bb * BR_W, (bb + 1) * BR_W)
        dt = _softplus(dt_ref[:, bb * DT_PAD:(bb + 1) * DT_PAD] + dtb_ref[...])
        cum_n = _chunk_cumsum(dt * a_ref[...], reverse)
        ex = _dot(jnp.concatenate(_split3(cum_n) + _split3(dt), axis=0), e_ref[...])
        cum_all = _sum3(ex[0:3 * tb], tb, 0)
        dt_all = _sum3(ex[3 * tb:6 * tb], tb, 0)
        for c in order:
            rs = slice(c * CHUNK, (c + 1) * CHUNK)
            xh = xs_ref[rs, xo:xo + BR_W]
            bm = xs_ref[rs, xo + BR_W:xo + BR_W + gn]
            cm = xs_ref[rs, xo + BR_W + gn:xo + BR_W + 2 * gn]
            cum = cum_all[rs, :]
            dt_e = dt_all[rs, :]
            cum_end = cum[0:1, :] if reverse else cum[CHUNK - 1:CHUNK, :]
            cum_row = jnp.sum(cum * teye, axis=0, keepdims=True)
            dt_row = jnp.sum(dt_e * teye, axis=0, keepdims=True)
            decay = jnp.exp(jnp.where(mask, cum - cum_row, -jnp.inf))
            bm8 = jnp.where(bmask, jnp.concatenate([bm] * M2_HEADS, axis=0), zero)
            scores = _dot_nt(cm, bm8)
            w = (scores * decay * dt_row).astype(BF16)
            xbd = jnp.where(bdmask, jnp.concatenate([xh] * M2_HEADS, axis=0), zero)
            st = st_ref[bb]
            y = _dot(w, xbd) + _dot(cm, st.astype(BF16)) * jnp.exp(cum)
            wx = (xh.astype(F32) * (jnp.exp(cum_end - cum) * dt_e)).astype(BF16)
            st_ref[bb] = jnp.exp(cum_end) * st + jnp.where(smask, _dot_tn(bm, wx), 0.0)
            if post:
                o_scr[rs, osl] = y
            else:
                o_ref[rs, osl] = y
        if post:
            y = of_ref[:, osl] + o_scr[:, osl] + dsk_ref[...] * xs_ref[:, xo:xo + BR_W].astype(F32)
            yz = y * _silu(z_ref[:, osl])
            ms = jnp.mean(yz * yz, axis=-1, keepdims=True)
            o_ref[:, osl] = (yz * lax.rsqrt(ms + EPS)) * nrm_ref[...]


def _ssd_call(uxs, udt, uz, of, st_in, dtb, a, e, dsk, nrm, reverse, name):
    tp = uxs.shape[0]
    tb = min(256, tp)
    nblk = tp // tb
    post = of is not None
    gn = M2_GROUPS * M2_STATE
    tix = (lambda i: nblk - 1 - i) if reverse else (lambda i: i)
    tile = lambda w: pl.BlockSpec((tb, BPS * w), lambda b, i: (tix(i), b))
    c2 = lambda b, i: (0, 0)
    stspec = pl.BlockSpec((BPS, gn, BR_W), lambda b, i: (b, 0, 0))
    in_specs = [tile(M2_XBC), tile(DT_PAD)]
    args = [uxs, udt]
    scratch = []
    if post:
        in_specs += [tile(BR_W), tile(BR_W)]
        args += [uz, of]
        scratch = [pltpu.VMEM((tb, BPS * BR_W), F32)]
    in_specs += [pl.BlockSpec((1, DT_PAD), c2), pl.BlockSpec((1, DT_PAD), c2),
                 pl.BlockSpec((DT_PAD, BR_W), c2)]
    args += [dtb, a, e]
    if post:
        in_specs += [pl.BlockSpec((1, BR_W), c2), pl.BlockSpec((1, BR_W), c2)]
        args += [dsk, nrm]
    in_specs.append(stspec)
    args.append(st_in)
    return pl.pallas_call(
        functools.partial(_ssd_kernel, reverse=reverse, post=post, tb=tb),
        out_shape=[jax.ShapeDtypeStruct((tp, NB * BR_W), F32),
                   jax.ShapeDtypeStruct(st_in.shape, F32)],
        grid=(NB // BPS, nblk),
        in_specs=in_specs,
        out_specs=[tile(BR_W), stspec],
        scratch_shapes=scratch,
        compiler_params=_cp(2),
        name=name,
    )(*args)


def _block_diag(blocks):
    n, a, b = blocks.shape
    eye = jnp.eye(n, dtype=blocks.dtype)
    return jnp.einsum('nab,nm->namb', blocks, eye).reshape(n * a, n * b)


def _s5_params(a_re, a_im, log_step, b_re, b_im, c_re, c_im):
    g, n, p = b_re.shape
    gpc = g // S5_NCH
    ar, ai = a_re.astype(F32), a_im.astype(F32)
    step = jnp.exp(log_step.astype(F32))[..., None]
    mag = jnp.exp(ar * step)
    ab_re, ab_im = mag * jnp.cos(ai * step), mag * jnp.sin(ai * step)
    den = ar * ar + ai * ai
    co_re = ((ab_re - 1.0) * ar + ab_im * ai) / den
    co_im = (ab_im * ar - (ab_re - 1.0) * ai) / den

    def chunked(m):
        return jax.vmap(_block_diag)(m.reshape(S5_NCH, gpc, m.shape[1], m.shape[2]))

    b_t = lambda m: jnp.swapaxes(m.astype(F32), 1, 2)
    bm = jnp.concatenate([chunked(b_t(b_re)), chunked(b_t(b_im))], axis=-1).astype(BF16)
    cr, ci = c_re.astype(F32), c_im.astype(F32)
    abs_, cms = [], []
    for d in range(2):
        ab = jnp.concatenate([ab_re[d].reshape(S5_NCH, gpc * n), ab_im[d].reshape(S5_NCH, gpc * n)],
                             axis=-1)
        abs_.append(jnp.broadcast_to(ab[:, None, :], (S5_NCH, NB, 2 * gpc * n)))
        cc_re = cr * co_re[d][:, None, :] - ci * co_im[d][:, None, :]
        cc_im = cr * co_im[d][:, None, :] + ci * co_re[d][:, None, :]
        to_rows = lambda m: chunked(jnp.swapaxes(m, 1, 2))
        cms.append(jnp.concatenate([to_rows(cc_re), to_rows(-cc_im)], axis=1).astype(BF16))
    return bm, abs_, cms


def _lane_vec(v, d):
    out = jnp.zeros((1, DT_PAD), F32)
    return lax.dynamic_update_slice(out, v.astype(F32)[None, :], (0, M2_HEADS * d))


def _head_expand(d):
    r = jnp.arange(DT_PAD)[:, None]
    c = jnp.arange(BR_W)[None, :] // M2_HEADDIM
    return (r == c + M2_HEADS * d).astype(BF16)


def kernel(x, c, ctx, c_ctx, norm_w, w_mod, b_mod, w_in, hg_lb_logits, hg_norm, s5_a_re, s5_a_im, s5_log_step, s5_b_re, s5_b_im, s5_c_re, s5_c_im, s5_d, s5_w_glu, s5_b_glu, lru_conv_w, lru_conv_b, lru_gate_w, lru_gate_b, lru_lam, m2_conv_w, m2_conv_b, m2_dt_bias, m2_a_log, m2_d, m2_norm, w_branch, w_gate, b_gate, w_out, final_norm):
    bsz, seq, dm = x.shape
    n_ctx = ctx.shape[1]
    depth = norm_w.shape[0]
    assert bsz == NB and seq % 256 == 0 and n_ctx % CHUNK == 0

    xl = jnp.transpose(x, (1, 0, 2)).reshape(seq * NB, dm)
    xc = jnp.transpose(ctx, (1, 0, 2)).reshape(n_ctx * NB, dm)
    c_all = jnp.concatenate([c, jnp.broadcast_to(c_ctx[None, :], (NB, dm))], axis=0)
    lb_all = jnp.cumsum(jax.nn.softmax(hg_lb_logits.astype(F32), axis=0), axis=0)
    perm = _tile_perm()
    perm_t = perm.T

    o_s5 = 5 * BR_W
    o_lru = o_s5 + 2 * BR_W
    o_lz = o_lru + BR_W
    o_m2 = o_lru + 2 * BR_W
    o_dt = o_m2 + M2_XBC
    o_mz = o_dt + 2 * M2_HEADS
    gn = M2_GROUPS * M2_STATE

    for l in range(depth):
        mod = _mod_call(c_all, w_mod[l].astype(BF16), b_mod[l][None, :])
        nw = norm_w[l][None, :]
        wl = w_in[l]
        w_s5 = wl[:, o_s5:o_lru].astype(BF16)
        w_lx = wl[:, o_lru:o_lz].astype(BF16)
        w_lz = wl[:, o_lz:o_m2].astype(BF16)
        w_mx = wl[:, o_m2:o_dt].astype(BF16)
        w_bm = jnp.concatenate(
            [wl[:, 0:o_s5], jnp.pad(wl[:, o_dt:o_mz], ((0, 0), (0, DT_PAD - 2 * M2_HEADS))), wl[:, o_mz:]],
            axis=1).astype(BF16)
        wg = w_gate[l].astype(BF16)
        bg = b_gate[l][:, None, :]
        wb = w_branch[l].astype(BF16)
        wo = w_out[l].astype(BF16)

        bm, s5_ab, s5_cm = _s5_params(s5_a_re[l], s5_a_im[l], s5_log_step[l], s5_b_re[l], s5_b_im[l],
                                      s5_c_re[l], s5_c_im[l])
        lru_wg = [jnp.concatenate([_block_diag(lru_gate_w[l, d, 0]), _block_diag(lru_gate_w[l, d, 1])],
                                  axis=1).astype(BF16) for d in range(2)]
        lru_bg = [lru_gate_b[l, d].reshape(1, 2 * BR_W) for d in range(2)]

        parts = []
        for name, xr, row0, period in (("ctx", xc, NB, n_ctx), ("lat", xl, 0, CHUNK)):
            sh = mod[row0:row0 + NB, 0:dm]
            sc = mod[row0:row0 + NB, dm:2 * dm]
            gt = mod[row0:row0 + NB, 2 * dm:3 * dm]
            ptiles = period // CHUNK
            ub, ucx, ucz = _proj_tm_call(xr, nw, sc, sh, w_s5, w_lx, w_lz, lru_conv_w[l],
                                         lru_conv_b[l][None, :], ptiles, "proj_tm_" + name)
            uxs, uq, uv, uff, ufb, uaz, udt, udz = _proj_bm_call(
                xr, nw, sc, sh, perm, w_mx, w_bm, (BR_W,) * 5 + (DT_PAD, BR_W),
                m2_conv_w[l], m2_conv_b[l][None, :], ptiles, "proj_bm_" + name)
            parts.append(dict(name=name, x=xr, sc=sc, sh=sh, gt=gt, uq=uq, uv=uv, uf=(uff, ufb), uaz=uaz,
                              ub=ub, ucx=ucx, ucz=ucz, uxs=uxs, udt=udt, udz=udz))

        ys = [dict(), dict()]
        for d, reverse in ((0, False), (1, True)):
            tag = "bwd" if reverse else "fwd"
            st_a = jnp.zeros((NB, HG_HEADS, HG_DK, HG_DK), F32)
            st_b = jnp.zeros((NB, 2 * (BR_W // S5_GROUP) * S5_STATE), F32)
            st_c = jnp.zeros((NB, BR_W), F32)
            st_d = jnp.zeros((NB, gn, BR_W), F32)
            for pi, p in enumerate(parts):
                nm = tag + "_" + p["name"]
                of = ys[pi] if reverse else dict(a=None, b=None, c=None, d=None)
                oa, st_a = _hgrn_call(p["uq"], p["uv"], p["uf"][d], p["uaz"], of["a"], st_a,
                                      lb_all[l, d][None, :], hg_norm[l][None, :], reverse, "hgrn_" + nm)
                ob, st_b = _s5_call(p["ub"], of["b"], st_b, bm, s5_ab[d], s5_cm[d], s5_d[l][None, :],
                                    s5_w_glu[l].astype(BF16), s5_b_glu[l][None, :], reverse, "s5_" + nm)
                oc, st_c = _lru_call(p["ucx"], p["ucz"], of["c"], st_c, lru_wg[d], lru_bg[d],
                                     lru_lam[l, d][None, :], reverse, "lru_" + nm)
                od, st_d = _ssd_call(p["uxs"], p["udt"], p["udz"], of["d"], st_d,
                                     _lane_vec(m2_dt_bias[l, d], d),
                                     _lane_vec(-jnp.exp(m2_a_log[l, d].astype(F32)), d), _head_expand(d),
                                     jnp.repeat(m2_d[l].astype(F32), M2_HEADDIM)[None, :],
                                     m2_norm[l][None, :], reverse, "ssd_" + nm)
                ys[pi] = dict(a=oa, b=ob, c=oc, d=od)

        last = l == depth - 1
        for pi, p in enumerate(parts):
            if last and p["name"] == "ctx":
                continue
            y = ys[pi]
            out = _merge_call(p["x"], (y["a"], y["b"], y["c"], y["d"]), nw, p["sc"], p["sh"], p["gt"],
                              perm_t, wg, bg, wb, wo, final_norm[None, :], last, "merge_" + p["name"])
            if p["name"] == "ctx":
                xc = out
            else:
                xl = out
    return jnp.transpose(xl.reshape(seq, NB, dm), (1, 0, 2))
```

```python
import functools
import math

import jax
import jax.numpy as jnp
from jax import lax
from jax.experimental import pallas as pl
from jax.experimental.pallas import tpu as pltpu

F32 = jnp.float32
BF16 = jnp.bfloat16

NB = 8
EPS = 1e-6
CHUNK = 64
TILE = CHUNK * NB
HALO = 2 * NB
CONV_W = 4
BR_W = 512
HG_HEADS = 4
HG_DK = BR_W // HG_HEADS
S5_GROUP = 16
S5_STATE = 64
S5_NCH = 4
LRU_C = 8.0
M2_HEADDIM = 64
M2_HEADS = BR_W // M2_HEADDIM
M2_GROUPS = 2
M2_STATE = 64
M2_XBC = BR_W + 2 * M2_GROUPS * M2_STATE
DT_PAD = 128
BPS = 2
VMEM_LIMIT = 56 * 1024 * 1024


def _cp(n_axes):
    return pltpu.CompilerParams(dimension_semantics=("arbitrary",) * n_axes,
                                vmem_limit_bytes=VMEM_LIMIT)


def _const_spec(shape):
    nd = len(shape)
    return pl.BlockSpec(shape, lambda *_: (0,) * nd, pipeline_mode=pl.Buffered(1))


def _dot(a, b):
    return jnp.dot(a, b, preferred_element_type=F32)


def _dot_nt(a, b):
    return lax.dot_general(a, b, (((1,), (1,)), ((), ())), preferred_element_type=F32)


def _dot_tn(a, b):
    return lax.dot_general(a, b, (((0,), (0,)), ((), ())), preferred_element_type=F32)


def _split3(x):
    hi = x.astype(BF16)
    r1 = x - hi.astype(F32)
    mid = r1.astype(BF16)
    lo = (r1 - mid.astype(F32)).astype(BF16)
    return hi, mid, lo


def _sum3(y, n, axis):
    if axis == 1:
        return (y[:, 0:n] + y[:, n:2 * n]) + y[:, 2 * n:3 * n]
    return (y[0:n] + y[n:2 * n]) + y[2 * n:3 * n]


def _chunk_cumsum(x, reverse):
    n = x.shape[0]
    i = lax.broadcasted_iota(jnp.int32, (n, n), 0)
    j = lax.broadcasted_iota(jnp.int32, (n, n), 1)
    tri = ((i // CHUNK == j // CHUNK) & ((j >= i) if reverse else (j <= i))).astype(BF16)
    y = _dot(tri, jnp.concatenate(_split3(x), axis=1))
    return _sum3(y, x.shape[1], 1)


def _sigmoid(x):
    return 0.5 * jnp.tanh(0.5 * x) + 0.5


def _silu(x):
    return x * _sigmoid(x)


def _softplus(x):
    return jnp.maximum(x, 0.0) + jnp.log1p(jnp.exp(-jnp.abs(x)))


def _adaln(x3, nw, sc, sh):
    nb, n, d = x3.shape
    ms = jnp.mean(x3 * x3, axis=-1, keepdims=True)
    xn = (x3 * lax.rsqrt(ms + EPS)) * nw
    h = xn * (1.0 + sc)[:, None, :] + sh[:, None, :]
    return h.reshape(nb * n, d)


def _tile_perm():
    i = jnp.arange(TILE)
    src = (i % CHUNK) * NB + i // CHUNK
    return (src[:, None] == jnp.arange(TILE)[None, :]).astype(BF16)


def _halo_select(first_step):
    i = jnp.arange(HALO)
    src = (i % NB) * NB + first_step + i // NB
    return (src[:, None] == jnp.arange(NB * NB)[None, :]).astype(BF16)


def _mod_kernel(c_ref, w_ref, b_ref, o_ref):
    s = _silu(c_ref[...]).astype(BF16)
    o_ref[...] = _dot(s, w_ref[...]) + b_ref[...]


def _mod_call(c_all, w, b):
    n, d = c_all.shape
    m = w.shape[1]
    return pl.pallas_call(
        _mod_kernel,
        out_shape=jax.ShapeDtypeStruct((n, m), F32),
        grid=(1,),
        in_specs=[pl.BlockSpec((n, d), lambda i: (0, 0)),
                  pl.BlockSpec((d, m), lambda i: (0, 0)),
                  pl.BlockSpec((1, m), lambda i: (0, 0))],
        out_specs=pl.BlockSpec((n, m), lambda i: (0, 0)),
        compiler_params=_cp(1),
        name="mod",
    )(c_all, w, b)


def _conv_tm(hb, halo, w_ref, cw_ref, cb_ref, ptiles):
    u = _dot(hb, w_ref[...])
    width = u.shape[1]
    if ptiles == 1:
        up = jnp.zeros((NB, width), F32)
        un = jnp.zeros((HALO, width), F32)
    else:
        i = pl.program_id(0)
        vp = (i % ptiles != 0).astype(F32)
        vn = (i % ptiles != ptiles - 1).astype(F32)
        up = _dot(halo[0], w_ref[...])[NB:HALO, :] * vp
        un = _dot(halo[1], w_ref[...]) * vn
    ue = jnp.concatenate([up, u, un], axis=0)
    xc = cb_ref[...] + cw_ref[0:1, :] * ue[0:TILE, :]
    for k in range(1, CONV_W):
        xc = xc + cw_ref[k:k + 1, :] * ue[k * NB:k * NB + TILE, :]
    return xc


def _store_bm(o_ref, u, wd, c0):
    cw = u.shape[1]
    for b in range(NB):
        o_ref[:, b * wd + c0:b * wd + c0 + cw] = u[b * CHUNK:(b + 1) * CHUNK, :].astype(o_ref.dtype)


def _proj_kernel(*refs, ptiles, widths):
    if ptiles == 1:
        x_ref, nw_ref, sc_ref, sh_ref, p_ref, pt_ref = refs[:6]
        rest = refs[6:]
        halo = None
    else:
        x_ref, xp_ref, xn_ref, nw_ref, sc_ref, sh_ref, p_ref, pt_ref, sp_ref, sn_ref = refs[:10]
        rest = refs[10:]
        halo = tuple(
            _dot(s_ref[...], _adaln(r[...], nw_ref[...], sc_ref[...], sh_ref[...]).astype(BF16)).astype(BF16)
            for s_ref, r in ((sp_ref, xp_ref), (sn_ref, xn_ref)))
    (w_s5_ref, w_lx_ref, w_lz_ref, cwl_ref, cbl_ref, w_x_ref, w_ref, cwm_ref, cbm_ref,
     ub_ref, ucx_ref, ucz_ref, uxs_ref) = rest[:13]
    o_refs = rest[13:]
    hb_bm = _adaln(x_ref[...], nw_ref[...], sc_ref[...], sh_ref[...]).astype(BF16)
    hb_tm = _dot(pt_ref[...], hb_bm).astype(BF16)
    ub_ref[...] = _dot(hb_tm, w_s5_ref[...])
    ucx_ref[...] = _conv_tm(hb_tm, halo, w_lx_ref, cwl_ref, cbl_ref, ptiles)
    ucz_ref[...] = _dot(hb_tm, w_lz_ref[...])
    xs = _silu(_conv_tm(hb_tm, halo, w_x_ref, cwm_ref, cbm_ref, ptiles)).astype(BF16)
    _store_bm(uxs_ref, _dot(p_ref[...], xs), M2_XBC, 0)
    off = 0
    for o_ref, wd in zip(o_refs, widths):
        for c0 in range(0, wd, BR_W):
            cw = min(BR_W, wd - c0)
            _store_bm(o_ref, _dot(hb_bm, w_ref[:, off + c0:off + c0 + cw]), wd, c0)
        off += wd


def _proj_call(x3, nw, sc, sh, perm, perm_t, w_s5, w_lx, w_lz, cwl, cbl, w_x, w, widths, cwm, cbm,
               ptiles, name):
    nb, t, d = x3.shape
    r = nb * t
    blk = lambda i: (0, i, 0)
    row = lambda i: (i, 0)
    in_specs = [pl.BlockSpec((NB, CHUNK, d), blk)]
    args = [x3]
    if ptiles > 1:
        per = CHUNK // NB
        last = t // NB - 1
        in_specs += [pl.BlockSpec((NB, NB, d), lambda i: (0, jnp.maximum(i * per - 1, 0), 0)),
                     pl.BlockSpec((NB, NB, d), lambda i: (0, jnp.minimum((i + 1) * per, last), 0))]
        args += [x3, x3]
    consts = [nw, sc, sh, perm, perm_t]
    if ptiles > 1:
        consts += [_halo_select(NB - 2), _halo_select(0)]
    consts += [w_s5, w_lx, w_lz, cwl, cbl, w_x, w, cwm, cbm]
    for a in consts:
        in_specs.append(_const_spec(a.shape))
        args.append(a)
    tm_w = (2 * BR_W, BR_W, BR_W)
    bm = [(M2_XBC, BF16)] + [(wd, F32) for wd in widths]
    return pl.pallas_call(
        functools.partial(_proj_kernel, ptiles=ptiles, widths=widths),
        out_shape=[jax.ShapeDtypeStruct((r, wd), F32) for wd in tm_w]
                  + [jax.ShapeDtypeStruct((t, NB * wd), dt) for wd, dt in bm],
        grid=(t // CHUNK,),
        in_specs=in_specs,
        out_specs=[pl.BlockSpec((TILE, wd), row) for wd in tm_w]
                  + [pl.BlockSpec((CHUNK, NB * wd), row) for wd, _ in bm],
        compiler_params=_cp(1),
        name=name,
    )(*args)


def _merge_kernel(x_ref, ya_ref, yb_ref, yc_ref, yd_ref, nw_ref, sc_ref, sh_ref, gt_ref, p_ref,
                  wg_ref, bg_ref, wb_ref, wo_ref, fn_ref, o_ref, *, final):
    x3 = x_ref[...]
    hb = _adaln(x3, nw_ref[...], sc_ref[...], sh_ref[...]).astype(BF16)

    def from_bm(y_ref):
        return jnp.concatenate([y_ref[:, b * BR_W:(b + 1) * BR_W] for b in range(NB)],
                               axis=0).astype(BF16)

    def from_tm(y_ref):
        return _dot(p_ref[...], y_ref[...].astype(BF16)).astype(BF16)

    ys = (from_bm(ya_ref), from_tm(yb_ref), from_tm(yc_ref), from_bm(yd_ref))
    m = None
    for k, y in enumerate(ys):
        g = _sigmoid(_dot(hb, wg_ref[k]) + bg_ref[k])
        p = _dot(y, wb_ref[k])
        m = g * p if m is None else m + g * p
    upd = _dot(m.astype(BF16), wo_ref[...])
    out = x3 + upd.reshape(x3.shape) * gt_ref[...][:, None, :]
    if final:
        ms = jnp.mean(out * out, axis=-1, keepdims=True)
        out = (out * lax.rsqrt(ms + EPS)) * fn_ref[...]
    o_ref[...] = out


def _merge_call(x3, ys, nw, sc, sh, gt, perm, wg, bg, wb, wo, fn, final, name):
    nb, t, d = x3.shape
    blk = lambda i: (0, i, 0)
    row = lambda i: (i, 0)
    tm_spec = pl.BlockSpec((TILE, BR_W), row)
    bm_spec = pl.BlockSpec((CHUNK, NB * BR_W), row)
    return pl.pallas_call(
        functools.partial(_merge_kernel, final=final),
        out_shape=jax.ShapeDtypeStruct(x3.shape, F32),
        grid=(t // CHUNK,),
        in_specs=[pl.BlockSpec((NB, CHUNK, d), blk), bm_spec, tm_spec, tm_spec, bm_spec,
                  _const_spec((1, d)), _const_spec((NB, d)), _const_spec((NB, d)),
                  _const_spec((NB, d)), _const_spec((TILE, TILE)),
                  _const_spec(wg.shape), _const_spec(bg.shape), _const_spec(wb.shape),
                  _const_spec(wo.shape), _const_spec((1, d))],
        out_specs=pl.BlockSpec((NB, CHUNK, d), blk),
        compiler_params=_cp(1),
        name=name,
    )(x3, *ys, nw, sc, sh, gt, perm, wg, bg, wb, wo, fn)


def _fused_call(parts, grid, name):
    n_in = [len(p["args"]) for p in parts]
    n_out = [len(p["out_shape"]) for p in parts]
    n_scr = [len(p["scratch"]) for p in parts]

    def kern(*refs):
        ins = refs[:sum(n_in)]
        outs = refs[sum(n_in):sum(n_in) + sum(n_out)]
        scr = refs[sum(n_in) + sum(n_out):]
        split = []
        i = o = s = 0
        for a, b, c in zip(n_in, n_out, n_scr):
            split.append((ins[i:i + a], outs[o:o + b], scr[s:s + c]))
            i, o, s = i + a, o + b, s + c

        @pl.when(pl.program_id(len(grid) - 1) == 0)
        def _():
            for p, (pin, pout, _) in zip(parts, split):
                pout[1][...] = pin[p["st_in"]][...]

        for p, (pin, pout, pscr) in zip(parts, split):
            p["kern"](*pin, *pout, *pscr)

    res = pl.pallas_call(
        kern,
        out_shape=[x for p in parts for x in p["out_shape"]],
        grid=grid,
        in_specs=[x for p in parts for x in p["in_specs"]],
        out_specs=[x for p in parts for x in p["out_specs"]],
        scratch_shapes=[x for p in parts for x in p["scratch"]],
        compiler_params=_cp(len(grid)),
        name=name,
    )(*[x for p in parts for x in p["args"]])
    out, o = [], 0
    for b in n_out:
        out.append(res[o:o + b])
        o += b
    return out


def _lru_kernel(*refs, reverse, post):
    if post:
        (x_ref, z_ref, of_ref, st_in_ref, wg_ref, bg_ref, lam_ref, o_ref, st_ref, a_scr, b_scr) = refs
    else:
        (x_ref, st_in_ref, wg_ref, bg_ref, lam_ref, o_ref, st_ref, a_scr, b_scr) = refs

    sp = _softplus(-lam_ref[...])
    xc = x_ref[...]
    gates = _sigmoid(_dot(xc.astype(BF16), wg_ref[...]) + bg_ref[...])
    log_a = (-LRU_C) * gates[:, 0:BR_W] * sp
    a = jnp.exp(log_a)
    a_scr[...] = a
    b_scr[...] = jnp.sqrt(-jnp.tanh(log_a) * (a * a + 1.0)) * (gates[:, BR_W:2 * BR_W] * xc)
    h = st_ref[...]
    for i in range(CHUNK):
        r0 = ((CHUNK - 1 - i) if reverse else i) * NB
        h = a_scr[r0:r0 + NB, :] * h + b_scr[r0:r0 + NB, :]
        b_scr[r0:r0 + NB, :] = h
    st_ref[...] = h
    if post:
        o_ref[...] = (of_ref[...] + b_scr[...]) * _silu(z_ref[...])
    else:
        o_ref[...] = b_scr[...]


def _lru_part(xc, z, of, st_in, wg, bg, lam, reverse):
    r = xc.shape[0]
    nblk = r // TILE
    post = of is not None
    tmap = (lambda i: (nblk - 1 - i, 0)) if reverse else (lambda i: (i, 0))
    c2 = lambda i: (0, 0)
    tile = pl.BlockSpec((TILE, BR_W), tmap)
    in_specs = [tile]
    args = [xc]
    if post:
        in_specs += [tile, tile]
        args += [z, of]
    in_specs += [pl.BlockSpec((NB, BR_W), c2), pl.BlockSpec((BR_W, 2 * BR_W), c2),
                 pl.BlockSpec((1, 2 * BR_W), c2), pl.BlockSpec((1, BR_W), c2)]
    args += [st_in, wg, bg, lam]
    return dict(
        kern=functools.partial(_lru_kernel, reverse=reverse, post=post),
        in_specs=in_specs, args=args, st_in=[a is st_in for a in args].index(True),
        out_shape=[jax.ShapeDtypeStruct((r, BR_W), F32), jax.ShapeDtypeStruct((NB, BR_W), F32)],
        out_specs=[tile, pl.BlockSpec((NB, BR_W), c2)],
        scratch=[pltpu.VMEM((TILE, BR_W), F32), pltpu.VMEM((TILE, BR_W), F32)])


def _gelu_tanh(x):
    return 0.5 * x * (1.0 + jnp.tanh(math.sqrt(2.0 / math.pi) * (x + 0.044715 * (x * x * x))))


def _s5_kernel(*refs, reverse, post):
    if post:
        (u_ref, of_ref, st_in_ref, bm_ref, ab_ref, cm_ref, dsk_ref, wglu_ref, bglu_ref,
         o_ref, st_ref, s_scr, y_scr) = refs
    else:
        (u_ref, st_in_ref, bm_ref, ab_ref, cm_ref, o_ref, st_ref, s_scr) = refs
        y_scr = o_ref

    half = (BR_W // S5_GROUP) * S5_STATE // S5_NCH
    cin = BR_W // S5_NCH
    for c in range(S5_NCH):
        ub = u_ref[:, c * cin:(c + 1) * cin].astype(BF16)
        s_scr[c] = _dot(ub, bm_ref[c])
    for c in range(S5_NCH):
        a_re = ab_ref[c, :, 0:half]
        a_im = ab_ref[c, :, half:2 * half]
        c0 = c * 2 * half
        s_re = st_ref[:, c0:c0 + half]
        s_im = st_ref[:, c0 + half:c0 + 2 * half]
        for i in range(CHUNK):
            r0 = ((CHUNK - 1 - i) if reverse else i) * NB
            n_re = a_re * s_re - a_im * s_im + s_scr[c, r0:r0 + NB, 0:half]
            n_im = a_re * s_im + a_im * s_re + s_scr[c, r0:r0 + NB, half:2 * half]
            s_scr[c, r0:r0 + NB, 0:half] = n_re
            s_scr[c, r0:r0 + NB, half:2 * half] = n_im
            s_re, s_im = n_re, n_im
        st_ref[:, c0:c0 + half] = s_re
        st_ref[:, c0 + half:c0 + 2 * half] = s_im
        y_scr[:, c * cin:(c + 1) * cin] = _dot(s_scr[c].astype(BF16), cm_ref[c])
    if post:
        u = u_ref[:, 0:BR_W]
        z = u_ref[:, BR_W:2 * BR_W]
        y = of_ref[...] + y_scr[...] + dsk_ref[...] * u
        g = _gelu_tanh(y)
        gl = _dot(g.astype(BF16), wglu_ref[...]) + bglu_ref[...]
        o_ref[...] = g * _sigmoid(gl) * _silu(z)


def _s5_part(u, of, st_in, bm, ab, cm, dsk, wglu, bglu, reverse):
    r = u.shape[0]
    nblk = r // TILE
    post = of is not None
    nst = st_in.shape[1]
    tmap = (lambda i: (nblk - 1 - i, 0)) if reverse else (lambda i: (i, 0))
    c2 = lambda i: (0, 0)
    c3 = lambda i: (0, 0, 0)
    in_specs = [pl.BlockSpec((TILE, 2 * BR_W), tmap)]
    args = [u]
    if post:
        in_specs.append(pl.BlockSpec((TILE, BR_W), tmap))
        args.append(of)
    in_specs += [pl.BlockSpec((NB, nst), c2), pl.BlockSpec(bm.shape, c3),
                 pl.BlockSpec(ab.shape, c3), pl.BlockSpec(cm.shape, c3)]
    args += [st_in, bm, ab, cm]
    scratch = [pltpu.VMEM((S5_NCH, TILE, nst // S5_NCH), F32)]
    if post:
        in_specs += [pl.BlockSpec((1, BR_W), c2), pl.BlockSpec((BR_W, BR_W), c2),
                     pl.BlockSpec((1, BR_W), c2)]
        args += [dsk, wglu, bglu]
        scratch.append(pltpu.VMEM((TILE, BR_W), F32))
    return dict(
        kern=functools.partial(_s5_kernel, reverse=reverse, post=post),
        in_specs=in_specs, args=args, st_in=[a is st_in for a in args].index(True),
        out_shape=[jax.ShapeDtypeStruct((r, BR_W), F32), jax.ShapeDtypeStruct((NB, nst), F32)],
        out_specs=[pl.BlockSpec((TILE, BR_W), tmap), pl.BlockSpec((NB, nst), c2)],
        scratch=scratch)


def _tri_mask(n, reverse):
    i = lax.broadcasted_iota(jnp.int32, (n, n), 0)
    j = lax.broadcasted_iota(jnp.int32, (n, n), 1)
    return (j >= i) if reverse else (j <= i)


def _hgrn_kernel(*refs, reverse, post, tb):
    if post:
        (q_ref, v_ref, f_ref, z_ref, of_ref, lb_ref, nrm_ref, st_in_ref,
         o_ref, st_ref, o_scr) = refs
    else:
        (q_ref, v_ref, f_ref, lb_ref, st_in_ref, o_ref, st_ref) = refs

    mask = _tri_mask(CHUNK, reverse)
    lb = lb_ref[...]
    nchunk = tb // CHUNK
    order = range(nchunk - 1, -1, -1) if reverse else range(nchunk)
    for bb in range(BPS):
        co = bb * BR_W
        f_all = lb + (1.0 - lb) * _sigmoid(f_ref[:, co:co + BR_W])
        b_all = _chunk_cumsum(jnp.log(f_all), reverse)
        for c in order:
            rs = slice(c * CHUNK, (c + 1) * CHUNK)
            k = 1.0 - f_all[rs, :]
            qs = _silu(q_ref[rs, co:co + BR_W])
            b = b_all[rs, :]
            b_end = b[0:1, :] if reverse else b[CHUNK - 1:CHUNK, :]
            mid = 0.5 * b_end
            qt = (qs * jnp.exp(b - mid)).astype(BF16)
            kt = (k * jnp.exp(mid - b)).astype(BF16)
            qb = (qs * jnp.exp(b)).astype(BF16)
            kb = (k * jnp.exp(b_end - b)).astype(BF16)
            dec = jnp.exp(b_end)
            vb = v_ref[rs, co:co + BR_W].astype(BF16)
            for h in range(HG_HEADS):
                sl = slice(h * HG_DK, (h + 1) * HG_DK)
                osl = slice(co + h * HG_DK, co + (h + 1) * HG_DK)
                att = jnp.where(mask, _dot_nt(qt[:, sl], kt[:, sl]), 0.0)
                st = st_ref[bb, h]
                o = _dot(att.astype(BF16), vb[:, sl]) + _dot_nt(qb[:, sl], st.astype(BF16))
                st_ref[bb, h] = dec[:, sl] * st + _dot_tn(vb[:, sl], kb[:, sl])
                if post:
                    o_scr[rs, osl] = o
                else:
                    o_ref[rs, osl] = o
        if post:
            nrm = nrm_ref[...]
            for h in range(HG_HEADS):
                sl = slice(h * HG_DK, (h + 1) * HG_DK)
                osl = slice(co + h * HG_DK, co + (h + 1) * HG_DK)
                o = of_ref[:, osl] + o_scr[:, osl]
                ms = jnp.mean(o * o, axis=-1, keepdims=True)
                o_ref[:, osl] = (o * lax.rsqrt(ms + EPS)) * nrm[:, sl] * _silu(z_ref[:, osl])


def _hgrn_part(uq, uv, uf, uz, of, st_in, lb, nrm, reverse):
    tp = uq.shape[0]
    tb = min(256, tp)
    nblk = tp // tb
    post = of is not None
    tix = (lambda i: nblk - 1 - i) if reverse else (lambda i: i)
    blk = pl.BlockSpec((tb, BPS * BR_W), lambda b, i: (tix(i), b))
    c2 = lambda b, i: (0, 0)
    stspec = pl.BlockSpec((BPS,) + st_in.shape[1:], lambda b, i: (b, 0, 0, 0))
    in_specs = [blk, blk, blk]
    args = [uq, uv, uf]
    scratch = []
    if post:
        in_specs += [blk, blk, pl.BlockSpec((1, BR_W), c2), pl.BlockSpec((1, BR_W), c2)]
        args += [uz, of, lb, nrm]
        scratch = [pltpu.VMEM((tb, BPS * BR_W), F32)]
    else:
        in_specs += [pl.BlockSpec((1, BR_W), c2)]
        args += [lb]
    in_specs.append(stspec)
    args.append(st_in)
    return dict(
        kern=functools.partial(_hgrn_kernel, reverse=reverse, post=post, tb=tb),
        in_specs=in_specs, args=args, st_in=[a is st_in for a in args].index(True),
        out_shape=[jax.ShapeDtypeStruct((tp, NB * BR_W), F32), jax.ShapeDtypeStruct(st_in.shape, F32)],
        out_specs=[blk, stspec],
        scratch=scratch)


def _ssd_kernel(*refs, reverse, post, tb):
    if post:
        (xs_ref, dt_ref, z_ref, of_ref, dtb_ref, a_ref, e_ref, dsk_ref, nrm_ref,
         st_in_ref, o_ref, st_ref, o_scr) = refs
    else:
        (xs_ref, dt_ref, dtb_ref, a_ref, e_ref, st_in_ref, o_ref, st_ref) = refs

    gn = M2_GROUPS * M2_STATE
    ri = lax.broadcasted_iota(jnp.int32, (CHUNK, BR_W), 0)
    cj = lax.broadcasted_iota(jnp.int32, (CHUNK, BR_W), 1) % CHUNK
    mask = (cj >= ri) if reverse else (cj <= ri)
    teye = (cj == ri).astype(F32)
    r8 = lax.broadcasted_iota(jnp.int32, (BR_W, BR_W), 0) // M2_HEADDIM
    c8 = lax.broadcasted_iota(jnp.int32, (BR_W, BR_W), 1) // M2_HEADDIM
    bdmask = r8 == c8
    hpg = M2_HEADS // M2_GROUPS
    g_row = lax.broadcasted_iota(jnp.int32, (BR_W, gn), 0) // (hpg * CHUNK)
    g_col = lax.broadcasted_iota(jnp.int32, (BR_W, gn), 1) // M2_STATE
    bmask = g_row == g_col
    s_row = lax.broadcasted_iota(jnp.int32, (gn, BR_W), 0) // M2_STATE
    s_col = lax.broadcasted_iota(jnp.int32, (gn, BR_W), 1) // (hpg * M2_HEADDIM)
    smask = s_row == s_col
    zero = jnp.zeros((), BF16)

    nchunk = tb // CHUNK
    order = range(nchunk - 1, -1, -1) if reverse else range(nchunk)
    for bb in range(BPS):
        xo = bb * M2_XBC
        osl = slice(bb * BR_W, (bb + 1) * BR_W)
        dt = _softplus(dt_ref[:, bb * DT_PAD:(bb + 1) * DT_PAD] + dtb_ref[...])
        cum_n = _chunk_cumsum(dt * a_ref[...], reverse)
        ex = _dot(jnp.concatenate(_split3(cum_n) + _split3(dt), axis=0), e_ref[...])
        cum_all = _sum3(ex[0:3 * tb], tb, 0)
        dt_all = _sum3(ex[3 * tb:6 * tb], tb, 0)
        for c in order:
            rs = slice(c * CHUNK, (c + 1) * CHUNK)
            xh = xs_ref[rs, xo:xo + BR_W]
            bm = xs_ref[rs, xo + BR_W:xo + BR_W + gn]
            cm = xs_ref[rs, xo + BR_W + gn:xo + BR_W + 2 * gn]
            cum = cum_all[rs, :]
            dt_e = dt_all[rs, :]
            cum_end = cum[0:1, :] if reverse else cum[CHUNK - 1:CHUNK, :]
            cum_row = jnp.sum(cum * teye, axis=0, keepdims=True)
            dt_row = jnp.sum(dt_e * teye, axis=0, keepdims=True)
            decay = jnp.exp(jnp.where(mask, cum - cum_row, -jnp.inf))
            bm8 = jnp.where(bmask, jnp.concatenate([bm] * M2_HEADS, axis=0), zero)
            scores = _dot_nt(cm, bm8)
            w = (scores * decay * dt_row).astype(BF16)
            xbd = jnp.where(bdmask, jnp.concatenate([xh] * M2_HEADS, axis=0), zero)
            st = st_ref[bb]
            y = _dot(w, xbd) + _dot(cm, st.astype(BF16)) * jnp.exp(cum)
            wx = (xh.astype(F32) * (jnp.exp(cum_end - cum) * dt_e)).astype(BF16)
            st_ref[bb] = jnp.exp(cum_end) * st + jnp.where(smask, _dot_tn(bm, wx), 0.0)
            if post:
                o_scr[rs, osl] = y
            else:
                o_ref[rs, osl] = y
        if post:
            y = of_ref[:, osl] + o_scr[:, osl] + dsk_ref[...] * xs_ref[:, xo:xo + BR_W].astype(F32)
            yz = y * _silu(z_ref[:, osl])
            ms = jnp.mean(yz * yz, axis=-1, keepdims=True)
            o_ref[:, osl] = (yz * lax.rsqrt(ms + EPS)) * nrm_ref[...]


def _ssd_part(uxs, udt, uz, of, st_in, dtb, a, e, dsk, nrm, reverse):
    tp = uxs.shape[0]
    tb = min(256, tp)
    nblk = tp // tb
    post = of is not None
    gn = M2_GROUPS * M2_STATE
    tix = (lambda i: nblk - 1 - i) if reverse else (lambda i: i)
    tile = lambda w: pl.BlockSpec((tb, BPS * w), lambda b, i: (tix(i), b))
    c2 = lambda b, i: (0, 0)
    stspec = pl.BlockSpec((BPS, gn, BR_W), lambda b, i: (b, 0, 0))
    in_specs = [tile(M2_XBC), tile(DT_PAD)]
    args = [uxs, udt]
    scratch = []
    if post:
        in_specs += [tile(BR_W), tile(BR_W)]
        args += [uz, of]
        scratch = [pltpu.VMEM((tb, BPS * BR_W), F32)]
    in_specs += [pl.BlockSpec((1, DT_PAD), c2), pl.BlockSpec((1, DT_PAD), c2),
                 pl.BlockSpec((DT_PAD, BR_W), c2)]
    args += [dtb, a, e]
    if post:
        in_specs += [pl.BlockSpec((1, BR_W), c2), pl.BlockSpec((1, BR_W), c2)]
        args += [dsk, nrm]
    in_specs.append(stspec)
    args.append(st_in)
    return dict(
        kern=functools.partial(_ssd_kernel, reverse=reverse, post=post, tb=tb),
        in_specs=in_specs, args=args, st_in=[a is st_in for a in args].index(True),
        out_shape=[jax.ShapeDtypeStruct((tp, NB * BR_W), F32), jax.ShapeDtypeStruct(st_in.shape, F32)],
        out_specs=[tile(BR_W), stspec],
        scratch=scratch)


def _block_diag(blocks):
    n, a, b = blocks.shape
    eye = jnp.eye(n, dtype=blocks.dtype)
    return jnp.einsum('nab,nm->namb', blocks, eye).reshape(n * a, n * b)


def _s5_params(a_re, a_im, log_step, b_re, b_im, c_re, c_im):
    g, n, p = b_re.shape
    gpc = g // S5_NCH
    ar, ai = a_re.astype(F32), a_im.astype(F32)
    step = jnp.exp(log_step.astype(F32))[..., None]
    mag = jnp.exp(ar * step)
    ab_re, ab_im = mag * jnp.cos(ai * step), mag * jnp.sin(ai * step)
    den = ar * ar + ai * ai
    co_re = ((ab_re - 1.0) * ar + ab_im * ai) / den
    co_im = (ab_im * ar - (ab_re - 1.0) * ai) / den

    def chunked(m):
        return jax.vmap(_block_diag)(m.reshape(S5_NCH, gpc, m.shape[1], m.shape[2]))

    b_t = lambda m: jnp.swapaxes(m.astype(F32), 1, 2)
    bm = jnp.concatenate([chunked(b_t(b_re)), chunked(b_t(b_im))], axis=-1).astype(BF16)
    cr, ci = c_re.astype(F32), c_im.astype(F32)
    abs_, cms = [], []
    for d in range(2):
        ab = jnp.concatenate([ab_re[d].reshape(S5_NCH, gpc * n), ab_im[d].reshape(S5_NCH, gpc * n)],
                             axis=-1)
        abs_.append(jnp.broadcast_to(ab[:, None, :], (S5_NCH, NB, 2 * gpc * n)))
        cc_re = cr * co_re[d][:, None, :] - ci * co_im[d][:, None, :]
        cc_im = cr * co_im[d][:, None, :] + ci * co_re[d][:, None, :]
        to_rows = lambda m: chunked(jnp.swapaxes(m, 1, 2))
        cms.append(jnp.concatenate([to_rows(cc_re), to_rows(-cc_im)], axis=1).astype(BF16))
    return bm, abs_, cms


def _lane_vec(v, d):
    out = jnp.zeros((1, DT_PAD), F32)
    return lax.dynamic_update_slice(out, v.astype(F32)[None, :], (0, M2_HEADS * d))


def _head_expand(d):
    r = jnp.arange(DT_PAD)[:, None]
    c = jnp.arange(BR_W)[None, :] // M2_HEADDIM
    return (r == c + M2_HEADS * d).astype(BF16)


def kernel(x, c, ctx, c_ctx, norm_w, w_mod, b_mod, w_in, hg_lb_logits, hg_norm, s5_a_re, s5_a_im, s5_log_step, s5_b_re, s5_b_im, s5_c_re, s5_c_im, s5_d, s5_w_glu, s5_b_glu, lru_conv_w, lru_conv_b, lru_gate_w, lru_gate_b, lru_lam, m2_conv_w, m2_conv_b, m2_dt_bias, m2_a_log, m2_d, m2_norm, w_branch, w_gate, b_gate, w_out, final_norm):
    bsz, seq, dm = x.shape
    n_ctx = ctx.shape[1]
    depth = norm_w.shape[0]
    assert bsz == NB and seq % 256 == 0 and n_ctx % CHUNK == 0

    xl, xc = x, ctx
    c_all = jnp.concatenate([c, jnp.broadcast_to(c_ctx[None, :], (NB, dm))], axis=0)
    lb_all = jnp.cumsum(jax.nn.softmax(hg_lb_logits.astype(F32), axis=0), axis=0)
    perm = _tile_perm()
    perm_t = perm.T

    o_s5 = 5 * BR_W
    o_lru = o_s5 + 2 * BR_W
    o_lz = o_lru + BR_W
    o_m2 = o_lru + 2 * BR_W
    o_dt = o_m2 + M2_XBC
    o_mz = o_dt + 2 * M2_HEADS
    gn = M2_GROUPS * M2_STATE

    for l in range(depth):
        mod = _mod_call(c_all, w_mod[l].astype(BF16), b_mod[l][None, :])
        nw = norm_w[l][None, :]
        wl = w_in[l]
        w_s5 = wl[:, o_s5:o_lru].astype(BF16)
        w_lx = wl[:, o_lru:o_lz].astype(BF16)
        w_lz = wl[:, o_lz:o_m2].astype(BF16)
        w_mx = wl[:, o_m2:o_dt].astype(BF16)
        w_bm = jnp.concatenate(
            [wl[:, 0:o_s5], jnp.pad(wl[:, o_dt:o_mz], ((0, 0), (0, DT_PAD - 2 * M2_HEADS))), wl[:, o_mz:]],
            axis=1).astype(BF16)
        wg = w_gate[l].astype(BF16)
        bg = b_gate[l][:, None, :]
        wb = w_branch[l].astype(BF16)
        wo = w_out[l].astype(BF16)

        bm, s5_ab, s5_cm = _s5_params(s5_a_re[l], s5_a_im[l], s5_log_step[l], s5_b_re[l], s5_b_im[l],
                                      s5_c_re[l], s5_c_im[l])
        lru_wg = [jnp.concatenate([_block_diag(lru_gate_w[l, d, 0]), _block_diag(lru_gate_w[l, d, 1])],
                                  axis=1).astype(BF16) for d in range(2)]
        lru_bg = [lru_gate_b[l, d].reshape(1, 2 * BR_W) for d in range(2)]

        parts = []
        for name, x3, row0, period in (("ctx", xc, NB, n_ctx), ("lat", xl, 0, CHUNK)):
            sh = mod[row0:row0 + NB, 0:dm]
            sc = mod[row0:row0 + NB, dm:2 * dm]
            gt = mod[row0:row0 + NB, 2 * dm:3 * dm]
            ub, ucx, ucz, uxs, uq, uv, uff, ufb, uaz, udt, udz = _proj_call(
                x3, nw, sc, sh, perm, perm_t, w_s5, w_lx, w_lz, lru_conv_w[l], lru_conv_b[l][None, :],
                w_mx, w_bm, (BR_W,) * 5 + (DT_PAD, BR_W), m2_conv_w[l], m2_conv_b[l][None, :],
                period // CHUNK, "proj_" + name)
            parts.append(dict(name=name, x=x3, sc=sc, sh=sh, gt=gt, uq=uq, uv=uv, uf=(uff, ufb), uaz=uaz,
                              ub=ub, ucx=ucx, ucz=ucz, uxs=uxs, udt=udt, udz=udz))

        ys = [dict(), dict()]
        for d, reverse in ((0, False), (1, True)):
            tag = "bwd" if reverse else "fwd"
            st_a = jnp.zeros((NB, HG_HEADS, HG_DK, HG_DK), F32)
            st_b = jnp.zeros((NB, 2 * (BR_W // S5_GROUP) * S5_STATE), F32)
            st_c = jnp.zeros((NB, BR_W), F32)
            st_d = jnp.zeros((NB, gn, BR_W), F32)
            for pi, p in enumerate(parts):
                nm = tag + "_" + p["name"]
                of = ys[pi] if reverse else dict(a=None, b=None, c=None, d=None)
                r = p["ub"].shape[0]
                tp = r // NB
                (ob, st_b), (oc, st_c) = _fused_call(
                    [_s5_part(p["ub"], of["b"], st_b, bm, s5_ab[d], s5_cm[d], s5_d[l][None, :],
                              s5_w_glu[l].astype(BF16), s5_b_glu[l][None, :], reverse),
                     _lru_part(p["ucx"], p["ucz"], of["c"], st_c, lru_wg[d], lru_bg[d],
                               lru_lam[l, d][None, :], reverse)],
                    (r // TILE,), "tm_" + nm)
                (oa, st_a), (od, st_d) = _fused_call(
                    [_hgrn_part(p["uq"], p["uv"], p["uf"][d], p["uaz"], of["a"], st_a,
                                lb_all[l, d][None, :], hg_norm[l][None, :], reverse),
                     _ssd_part(p["uxs"], p["udt"], p["udz"], of["d"], st_d,
                               _lane_vec(m2_dt_bias[l, d], d),
                               _lane_vec(-jnp.exp(m2_a_log[l, d].astype(F32)), d), _head_expand(d),
                               jnp.repeat(m2_d[l].astype(F32), M2_HEADDIM)[None, :],
                               m2_norm[l][None, :], reverse)],
                    (NB // BPS, tp // min(256, tp)), "bm_" + nm)
                ys[pi] = dict(a=oa, b=ob, c=oc, d=od)

        last = l == depth - 1
        for pi, p in enumerate(parts):
            if last and p["name"] == "ctx":
                continue
            y = ys[pi]
            out = _merge_call(p["x"], (y["a"], y["b"], y["c"], y["d"]), nw, p["sc"], p["sh"], p["gt"],
                              perm, wg, bg, wb, wo, final_norm[None, :], last, "merge_" + p["name"])
            if p["name"] == "ctx":
                xc = out
            else:
                xl = out
    return xl
```

```python
import functools
import math

import jax
import jax.numpy as jnp
from jax import lax
from jax.experimental import pallas as pl
from jax.experimental.pallas import tpu as pltpu

F32 = jnp.float32
BF16 = jnp.bfloat16

NB = 8
EPS = 1e-6
CHUNK = 64
TILE = CHUNK * NB
HALO = 2 * NB
CONV_W = 4
BR_W = 512
HG_HEADS = 4
HG_DK = BR_W // HG_HEADS
S5_GROUP = 16
S5_STATE = 64
S5_NCH = 4
S5_BLK = 4
LRU_C = 8.0
LRU_HALVES = 2
M2_HEADDIM = 64
M2_HEADS = BR_W // M2_HEADDIM
M2_GROUPS = 2
M2_STATE = 64
M2_XBC = BR_W + 2 * M2_GROUPS * M2_STATE
DT_PAD = 128
BPS = 2
VMEM_LIMIT = 56 * 1024 * 1024


def _cp(n_axes):
    return pltpu.CompilerParams(dimension_semantics=("arbitrary",) * n_axes,
                                vmem_limit_bytes=VMEM_LIMIT)


def _const_spec(shape):
    nd = len(shape)
    return pl.BlockSpec(shape, lambda *_: (0,) * nd, pipeline_mode=pl.Buffered(1))


def _dot(a, b):
    return jnp.dot(a, b, preferred_element_type=F32)


def _dot_nt(a, b):
    return lax.dot_general(a, b, (((1,), (1,)), ((), ())), preferred_element_type=F32)


def _dot_tn(a, b):
    return lax.dot_general(a, b, (((0,), (0,)), ((), ())), preferred_element_type=F32)


def _split3(x):
    hi = x.astype(BF16)
    r1 = x - hi.astype(F32)
    mid = r1.astype(BF16)
    lo = (r1 - mid.astype(F32)).astype(BF16)
    return hi, mid, lo


def _sum3(y, n, axis):
    if axis == 1:
        return (y[:, 0:n] + y[:, n:2 * n]) + y[:, 2 * n:3 * n]
    return (y[0:n] + y[n:2 * n]) + y[2 * n:3 * n]


def _chunk_cumsum(x, reverse):
    n = x.shape[0]
    i = lax.broadcasted_iota(jnp.int32, (n, n), 0)
    j = lax.broadcasted_iota(jnp.int32, (n, n), 1)
    tri = ((i // CHUNK == j // CHUNK) & ((j >= i) if reverse else (j <= i))).astype(BF16)
    y = _dot(tri, jnp.concatenate(_split3(x), axis=1))
    return _sum3(y, x.shape[1], 1)


def _sigmoid(x):
    return 0.5 * jnp.tanh(0.5 * x) + 0.5


def _silu(x):
    return x * _sigmoid(x)


def _softplus(x):
    return jnp.maximum(x, 0.0) + jnp.log1p(jnp.exp(-jnp.abs(x)))


def _adaln(x3, nw, sc, sh):
    nb, n, d = x3.shape
    ms = jnp.mean(x3 * x3, axis=-1, keepdims=True)
    xn = (x3 * lax.rsqrt(ms + EPS)) * nw
    h = xn * (1.0 + sc)[:, None, :] + sh[:, None, :]
    return h.reshape(nb * n, d)


def _tile_perm():
    i = jnp.arange(TILE)
    src = (i % CHUNK) * NB + i // CHUNK
    return (src[:, None] == jnp.arange(TILE)[None, :]).astype(BF16)


def _halo_select(first_step):
    i = jnp.arange(HALO)
    src = (i % NB) * NB + first_step + i // NB
    return (src[:, None] == jnp.arange(NB * NB)[None, :]).astype(BF16)


def _mod_kernel(c_ref, w_ref, b_ref, o_ref):
    s = _silu(c_ref[...]).astype(BF16)
    o_ref[...] = _dot(s, w_ref[...]) + b_ref[...]


def _mod_call(c_all, w, b):
    n, d = c_all.shape
    m = w.shape[1]
    return pl.pallas_call(
        _mod_kernel,
        out_shape=jax.ShapeDtypeStruct((n, m), F32),
        grid=(1,),
        in_specs=[pl.BlockSpec((n, d), lambda i: (0, 0)),
                  pl.BlockSpec((d, m), lambda i: (0, 0)),
                  pl.BlockSpec((1, m), lambda i: (0, 0))],
        out_specs=pl.BlockSpec((n, m), lambda i: (0, 0)),
        compiler_params=_cp(1),
        name="mod",
    )(c_all, w, b)


def _conv_tm(hb, halo, w_ref, cw_ref, cb_ref, ptiles):
    u = _dot(hb, w_ref[...])
    width = u.shape[1]
    if ptiles == 1:
        up = jnp.zeros((NB, width), F32)
        un = jnp.zeros((HALO, width), F32)
    else:
        i = pl.program_id(0)
        vp = (i % ptiles != 0).astype(F32)
        vn = (i % ptiles != ptiles - 1).astype(F32)
        up = _dot(halo[0], w_ref[...])[NB:HALO, :] * vp
        un = _dot(halo[1], w_ref[...]) * vn
    ue = jnp.concatenate([up, u, un], axis=0)
    xc = cb_ref[...] + cw_ref[0:1, :] * ue[0:TILE, :]
    for k in range(1, CONV_W):
        xc = xc + cw_ref[k:k + 1, :] * ue[k * NB:k * NB + TILE, :]
    return xc


def _store_bm(o_ref, u, wd, c0):
    cw = u.shape[1]
    for b in range(NB):
        o_ref[:, b * wd + c0:b * wd + c0 + cw] = u[b * CHUNK:(b + 1) * CHUNK, :].astype(o_ref.dtype)


def _proj_kernel(*refs, ptiles, widths):
    if ptiles == 1:
        x_ref, nw_ref, sc_ref, sh_ref, p_ref, pt_ref = refs[:6]
        rest = refs[6:]
        halo = None
    else:
        x_ref, xp_ref, xn_ref, nw_ref, sc_ref, sh_ref, p_ref, pt_ref, sp_ref, sn_ref = refs[:10]
        rest = refs[10:]
        halo = tuple(
            _dot(s_ref[...], _adaln(r[...], nw_ref[...], sc_ref[...], sh_ref[...]).astype(BF16)).astype(BF16)
            for s_ref, r in ((sp_ref, xp_ref), (sn_ref, xn_ref)))
    (w_s5_ref, w_lx_ref, w_lz_ref, cwl_ref, cbl_ref, w_x_ref, w_ref, cwm_ref, cbm_ref,
     ub_ref, ucx_ref, ucz_ref, uxs_ref) = rest[:13]
    o_refs = rest[13:]
    hb_bm = _adaln(x_ref[...], nw_ref[...], sc_ref[...], sh_ref[...]).astype(BF16)
    hb_tm = _dot(pt_ref[...], hb_bm).astype(BF16)
    ub_ref[...] = _dot(hb_tm, w_s5_ref[...])
    ucx_ref[...] = _conv_tm(hb_tm, halo, w_lx_ref, cwl_ref, cbl_ref, ptiles)
    ucz_ref[...] = _dot(hb_tm, w_lz_ref[...])
    xs = _silu(_conv_tm(hb_tm, halo, w_x_ref, cwm_ref, cbm_ref, ptiles)).astype(BF16)
    _store_bm(uxs_ref, _dot(p_ref[...], xs), M2_XBC, 0)
    off = 0
    for o_ref, wd in zip(o_refs, widths):
        for c0 in range(0, wd, BR_W):
            cw = min(BR_W, wd - c0)
            _store_bm(o_ref, _dot(hb_bm, w_ref[:, off + c0:off + c0 + cw]), wd, c0)
        off += wd


def _proj_call(x3, nw, sc, sh, perm, perm_t, w_s5, w_lx, w_lz, cwl, cbl, w_x, w, widths, cwm, cbm,
               ptiles, name):
    nb, t, d = x3.shape
    r = nb * t
    blk = lambda i: (0, i, 0)
    row = lambda i: (i, 0)
    in_specs = [pl.BlockSpec((NB, CHUNK, d), blk)]
    args = [x3]
    if ptiles > 1:
        per = CHUNK // NB
        last = t // NB - 1
        in_specs += [pl.BlockSpec((NB, NB, d), lambda i: (0, jnp.maximum(i * per - 1, 0), 0)),
                     pl.BlockSpec((NB, NB, d), lambda i: (0, jnp.minimum((i + 1) * per, last), 0))]
        args += [x3, x3]
    consts = [nw, sc, sh, perm, perm_t]
    if ptiles > 1:
        consts += [_halo_select(NB - 2), _halo_select(0)]
    consts += [w_s5, w_lx, w_lz, cwl, cbl, w_x, w, cwm, cbm]
    for a in consts:
        in_specs.append(_const_spec(a.shape))
        args.append(a)
    tm_w = (2 * BR_W, BR_W, BR_W)
    bm = [(M2_XBC, BF16)] + [(wd, F32) for wd in widths]
    return pl.pallas_call(
        functools.partial(_proj_kernel, ptiles=ptiles, widths=widths),
        out_shape=[jax.ShapeDtypeStruct((r, wd), F32) for wd in tm_w]
                  + [jax.ShapeDtypeStruct((t, NB * wd), dt) for wd, dt in bm],
        grid=(t // CHUNK,),
        in_specs=in_specs,
        out_specs=[pl.BlockSpec((TILE, wd), row) for wd in tm_w]
                  + [pl.BlockSpec((CHUNK, NB * wd), row) for wd, _ in bm],
        compiler_params=_cp(1),
        name=name,
    )(*args)


def _merge_kernel(x_ref, ya_ref, yb_ref, yc_ref, yd_ref, nw_ref, sc_ref, sh_ref, gt_ref, p_ref,
                  wg_ref, bg_ref, wb_ref, wo_ref, fn_ref, o_ref, *, final):
    x3 = x_ref[...]
    hb = _adaln(x3, nw_ref[...], sc_ref[...], sh_ref[...]).astype(BF16)

    def from_bm(y_ref):
        return jnp.concatenate([y_ref[:, b * BR_W:(b + 1) * BR_W] for b in range(NB)],
                               axis=0).astype(BF16)

    def from_tm(y_ref):
        return _dot(p_ref[...], y_ref[...].astype(BF16)).astype(BF16)

    ys = (from_bm(ya_ref), from_tm(yb_ref), from_tm(yc_ref), from_bm(yd_ref))
    m = None
    for k, y in enumerate(ys):
        g = _sigmoid(_dot(hb, wg_ref[k]) + bg_ref[k])
        p = _dot(y, wb_ref[k])
        m = g * p if m is None else m + g * p
    upd = _dot(m.astype(BF16), wo_ref[...])
    out = x3 + upd.reshape(x3.shape) * gt_ref[...][:, None, :]
    if final:
        ms = jnp.mean(out * out, axis=-1, keepdims=True)
        out = (out * lax.rsqrt(ms + EPS)) * fn_ref[...]
    o_ref[...] = out


def _merge_call(x3, ys, nw, sc, sh, gt, perm, wg, bg, wb, wo, fn, final, name):
    nb, t, d = x3.shape
    blk = lambda i: (0, i, 0)
    row = lambda i: (i, 0)
    tm_spec = pl.BlockSpec((TILE, BR_W), row)
    bm_spec = pl.BlockSpec((CHUNK, NB * BR_W), row)
    return pl.pallas_call(
        functools.partial(_merge_kernel, final=final),
        out_shape=jax.ShapeDtypeStruct(x3.shape, F32),
        grid=(t // CHUNK,),
        in_specs=[pl.BlockSpec((NB, CHUNK, d), blk), bm_spec, tm_spec, tm_spec, bm_spec,
                  _const_spec((1, d)), _const_spec((NB, d)), _const_spec((NB, d)),
                  _const_spec((NB, d)), _const_spec((TILE, TILE)),
                  _const_spec(wg.shape), _const_spec(bg.shape), _const_spec(wb.shape),
                  _const_spec(wo.shape), _const_spec((1, d))],
        out_specs=pl.BlockSpec((NB, CHUNK, d), blk),
        compiler_params=_cp(1),
        name=name,
    )(x3, *ys, nw, sc, sh, gt, perm, wg, bg, wb, wo, fn)


def _fused_call(parts, grid, name):
    n_in = [len(p["args"]) for p in parts]
    n_out = [len(p["out_shape"]) for p in parts]
    n_scr = [len(p["scratch"]) for p in parts]

    def kern(*refs):
        ins = refs[:sum(n_in)]
        outs = refs[sum(n_in):sum(n_in) + sum(n_out)]
        scr = refs[sum(n_in) + sum(n_out):]
        split = []
        i = o = s = 0
        for a, b, c in zip(n_in, n_out, n_scr):
            split.append((ins[i:i + a], outs[o:o + b], scr[s:s + c]))
            i, o, s = i + a, o + b, s + c

        @pl.when(pl.program_id(len(grid) - 1) == 0)
        def _():
            for p, (pin, pout, _) in zip(parts, split):
                pout[1][...] = pin[p["st_in"]][...]

        active = [p["kern"](*pin, *pout, *pscr) for p, (pin, pout, pscr) in zip(parts, split)]
        while active:
            for g in list(active):
                if next(g, StopIteration) is StopIteration:
                    active.remove(g)

    res = pl.pallas_call(
        kern,
        out_shape=[x for p in parts for x in p["out_shape"]],
        grid=grid,
        in_specs=[x for p in parts for x in p["in_specs"]],
        out_specs=[x for p in parts for x in p["out_specs"]],
        scratch_shapes=[x for p in parts for x in p["scratch"]],
        compiler_params=_cp(len(grid)),
        name=name,
    )(*[x for p in parts for x in p["args"]])
    out, o = [], 0
    for b in n_out:
        out.append(res[o:o + b])
        o += b
    return out


def _lru_kernel(*refs, reverse, post):
    if post:
        (x_ref, z_ref, of_ref, st_in_ref, wg_ref, bg_ref, lam_ref, o_ref, st_ref, a_scr, b_scr) = refs
    else:
        (x_ref, st_in_ref, wg_ref, bg_ref, lam_ref, o_ref, st_ref, a_scr, b_scr) = refs

    c1 = (-0.5 * LRU_C) * _softplus(-lam_ref[...])
    nq = 4
    qr = TILE // nq
    hw = BR_W // LRU_HALVES
    for q in (range(nq - 1, -1, -1) if reverse else range(nq)):
        rs = slice(q * qr, (q + 1) * qr)
        for j in range(LRU_HALVES):
            cs = slice(j * hw, (j + 1) * hw)
            xc = x_ref[rs, cs]
            th = jnp.tanh(_dot(xc.astype(BF16), wg_ref[j]) + bg_ref[j])
            log_a = c1[:, cs] * th[:, 0:hw] + c1[:, cs]
            a = jnp.exp(log_a)
            a_scr[rs, cs] = a
            v = -jnp.tanh(log_a) * (a * a + 1.0)
            root = jnp.where(v > 0.0, v * lax.rsqrt(v), 0.0)
            b_scr[rs, cs] = root * ((0.5 * th[:, hw:2 * hw] + 0.5) * xc)
        yield
    h = st_ref[...]
    for i in range(CHUNK):
        r0 = ((CHUNK - 1 - i) if reverse else i) * NB
        h = a_scr[r0:r0 + NB, :] * h + b_scr[r0:r0 + NB, :]
        b_scr[r0:r0 + NB, :] = h
        yield
    st_ref[...] = h
    if post:
        o_ref[...] = (of_ref[...] + b_scr[...]) * _silu(z_ref[...])
    else:
        o_ref[...] = b_scr[...]


def _lru_part(xc, z, of, st_in, wg, bg, lam, reverse):
    r = xc.shape[0]
    nblk = r // TILE
    post = of is not None
    tmap = (lambda i: (nblk - 1 - i, 0)) if reverse else (lambda i: (i, 0))
    c2 = lambda i: (0, 0)
    tile = pl.BlockSpec((TILE, BR_W), tmap)
    in_specs = [tile]
    args = [xc]
    if post:
        in_specs += [tile, tile]
        args += [z, of]
    c3 = lambda i: (0, 0, 0)
    in_specs += [pl.BlockSpec((NB, BR_W), c2), pl.BlockSpec(wg.shape, c3),
                 pl.BlockSpec(bg.shape, c3), pl.BlockSpec((1, BR_W), c2)]
    args += [st_in, wg, bg, lam]
    return dict(
        kern=functools.partial(_lru_kernel, reverse=reverse, post=post),
        in_specs=in_specs, args=args, st_in=[a is st_in for a in args].index(True),
        out_shape=[jax.ShapeDtypeStruct((r, BR_W), F32), jax.ShapeDtypeStruct((NB, BR_W), F32)],
        out_specs=[tile, pl.BlockSpec((NB, BR_W), c2)],
        scratch=[pltpu.VMEM((TILE, BR_W), F32), pltpu.VMEM((TILE, BR_W), F32)])


def _gelu_tanh(x):
    return 0.5 * x * (1.0 + jnp.tanh(math.sqrt(2.0 / math.pi) * (x + 0.044715 * (x * x * x))))


def _s5_kernel(*refs, reverse, post):
    if post:
        (u_ref, of_ref, st_in_ref, wd_ref, ab_ref, wy_ref, wf_ref, dsk_ref, wglu_ref, bglu_ref,
         o_ref, st_ref, s_scr, y_scr) = refs
    else:
        (u_ref, st_in_ref, wd_ref, ab_ref, wy_ref, wf_ref, o_ref, st_ref, s_scr) = refs
        y_scr = o_ref

    half = (BR_W // S5_GROUP) * S5_STATE // S5_NCH
    cin = BR_W // S5_NCH
    nk = CHUNK // S5_BLK
    u4s = []
    for c in range(S5_NCH):
        ug = u_ref[:, c * cin:(c + 1) * cin].reshape(nk, S5_BLK, NB, cin)
        u4 = jnp.concatenate([ug[:, i].reshape(nk * NB, cin) for i in range(S5_BLK)], axis=1).astype(BF16)
        u4s.append(u4)
        s_scr[c] = _dot(u4, wd_ref[c])
        yield
    for c in range(S5_NCH):
        a_re = ab_ref[c, :, 0:half]
        a_im = ab_ref[c, :, half:2 * half]
        c0 = c * 2 * half
        s_re = st_ref[:, c0:c0 + half]
        s_im = st_ref[:, c0 + half:c0 + 2 * half]
        for kk in range(nk):
            r0 = ((nk - 1 - kk) if reverse else kk) * NB
            n_re = a_re * s_re - a_im * s_im + s_scr[c, r0:r0 + NB, 0:half]
            n_im = a_re * s_im + a_im * s_re + s_scr[c, r0:r0 + NB, half:2 * half]
            s_scr[c, r0:r0 + NB, 0:half] = s_re
            s_scr[c, r0:r0 + NB, half:2 * half] = s_im
            s_re, s_im = n_re, n_im
            yield
        st_ref[:, c0:c0 + half] = s_re
        st_ref[:, c0 + half:c0 + 2 * half] = s_im
        y4 = _dot(s_scr[c].astype(BF16), wy_ref[c]) + _dot(u4s[c], wf_ref[c])
        y = jnp.stack([y4[:, i * cin:(i + 1) * cin].reshape(nk, NB, cin) for i in range(S5_BLK)], axis=1)
        y_scr[:, c * cin:(c + 1) * cin] = y.reshape(TILE, cin)
        yield
    if post:
        u = u_ref[:, 0:BR_W]
        z = u_ref[:, BR_W:2 * BR_W]
        y = of_ref[...] + y_scr[...] + dsk_ref[...] * u
        g = _gelu_tanh(y)
        gl = _dot(g.astype(BF16), wglu_ref[...]) + bglu_ref[...]
        o_ref[...] = g * _sigmoid(gl) * _silu(z)


def _s5_part(u, of, st_in, wd, ab, wy, wf, dsk, wglu, bglu, reverse):
    r = u.shape[0]
    nblk = r // TILE
    post = of is not None
    nst = st_in.shape[1]
    tmap = (lambda i: (nblk - 1 - i, 0)) if reverse else (lambda i: (i, 0))
    c2 = lambda i: (0, 0)
    c3 = lambda i: (0, 0, 0)
    in_specs = [pl.BlockSpec((TILE, 2 * BR_W), tmap)]
    args = [u]
    if post:
        in_specs.append(pl.BlockSpec((TILE, BR_W), tmap))
        args.append(of)
    in_specs += [pl.BlockSpec((NB, nst), c2)] + [pl.BlockSpec(a.shape, c3) for a in (wd, ab, wy, wf)]
    args += [st_in, wd, ab, wy, wf]
    scratch = [pltpu.VMEM((S5_NCH, TILE // S5_BLK, nst // S5_NCH), F32)]
    if post:
        in_specs += [pl.BlockSpec((1, BR_W), c2), pl.BlockSpec((BR_W, BR_W), c2),
                     pl.BlockSpec((1, BR_W), c2)]
        args += [dsk, wglu, bglu]
        scratch.append(pltpu.VMEM((TILE, BR_W), F32))
    return dict(
        kern=functools.partial(_s5_kernel, reverse=reverse, post=post),
        in_specs=in_specs, args=args, st_in=[a is st_in for a in args].index(True),
        out_shape=[jax.ShapeDtypeStruct((r, BR_W), F32), jax.ShapeDtypeStruct((NB, nst), F32)],
        out_specs=[pl.BlockSpec((TILE, BR_W), tmap), pl.BlockSpec((NB, nst), c2)],
        scratch=scratch)


def _tri_mask(n, reverse):
    i = lax.broadcasted_iota(jnp.int32, (n, n), 0)
    j = lax.broadcasted_iota(jnp.int32, (n, n), 1)
    return (j >= i) if reverse else (j <= i)


def _hgrn_kernel(*refs, reverse, post, tb):
    if post:
        (q_ref, v_ref, f_ref, z_ref, of_ref, lb_ref, nrm_ref, st_in_ref,
         o_ref, st_ref, o_scr) = refs
    else:
        (q_ref, v_ref, f_ref, lb_ref, st_in_ref, o_ref, st_ref) = refs

    mask = _tri_mask(CHUNK, reverse)
    lb = lb_ref[...]
    nchunk = tb // CHUNK
    order = range(nchunk - 1, -1, -1) if reverse else range(nchunk)
    for bb in range(BPS):
        co = bb * BR_W
        f_all = lb + (1.0 - lb) * _sigmoid(f_ref[:, co:co + BR_W])
        b_all = _chunk_cumsum(jnp.log(f_all), reverse)
        for c in order:
            rs = slice(c * CHUNK, (c + 1) * CHUNK)
            k = 1.0 - f_all[rs, :]
            qs = _silu(q_ref[rs, co:co + BR_W])
            b = b_all[rs, :]
            b_end = b[0:1, :] if reverse else b[CHUNK - 1:CHUNK, :]
            mid = 0.5 * b_end
            qt = (qs * jnp.exp(b - mid)).astype(BF16)
            kt = (k * jnp.exp(mid - b)).astype(BF16)
            qb = (qs * jnp.exp(b)).astype(BF16)
            kb = (k * jnp.exp(b_end - b)).astype(BF16)
            dec = jnp.exp(b_end)
            vb = v_ref[rs, co:co + BR_W].astype(BF16)
            for h in range(HG_HEADS):
                sl = slice(h * HG_DK, (h + 1) * HG_DK)
                osl = slice(co + h * HG_DK, co + (h + 1) * HG_DK)
                att = jnp.where(mask, _dot_nt(qt[:, sl], kt[:, sl]), 0.0)
                st = st_ref[bb, h]
                o = _dot(att.astype(BF16), vb[:, sl]) + _dot_nt(qb[:, sl], st.astype(BF16))
                st_ref[bb, h] = dec[:, sl] * st + _dot_tn(vb[:, sl], kb[:, sl])
                if post:
                    o_scr[rs, osl] = o
                else:
                    o_ref[rs, osl] = o
            yield
        if post:
            nrm = nrm_ref[...]
            for h in range(HG_HEADS):
                sl = slice(h * HG_DK, (h + 1) * HG_DK)
                osl = slice(co + h * HG_DK, co + (h + 1) * HG_DK)
                o = of_ref[:, osl] + o_scr[:, osl]
                ms = jnp.mean(o * o, axis=-1, keepdims=True)
                o_ref[:, osl] = (o * lax.rsqrt(ms + EPS)) * nrm[:, sl] * _silu(z_ref[:, osl])


def _hgrn_part(uq, uv, uf, uz, of, st_in, lb, nrm, reverse):
    tp = uq.shape[0]
    tb = min(256, tp)
    nblk = tp // tb
    post = of is not None
    tix = (lambda i: nblk - 1 - i) if reverse else (lambda i: i)
    blk = pl.BlockSpec((tb, BPS * BR_W), lambda b, i: (tix(i), b))
    c2 = lambda b, i: (0, 0)
    stspec = pl.BlockSpec((BPS,) + st_in.shape[1:], lambda b, i: (b, 0, 0, 0))
    in_specs = [blk, blk, blk]
    args = [uq, uv, uf]
    scratch = []
    if post:
        in_specs += [blk, blk, pl.BlockSpec((1, BR_W), c2), pl.BlockSpec((1, BR_W), c2)]
        args += [uz, of, lb, nrm]
        scratch = [pltpu.VMEM((tb, BPS * BR_W), F32)]
    else:
        in_specs += [pl.BlockSpec((1, BR_W), c2)]
        args += [lb]
    in_specs.append(stspec)
    args.append(st_in)
    return dict(
        kern=functools.partial(_hgrn_kernel, reverse=reverse, post=post, tb=tb),
        in_specs=in_specs, args=args, st_in=[a is st_in for a in args].index(True),
        out_shape=[jax.ShapeDtypeStruct((tp, NB * BR_W), F32), jax.ShapeDtypeStruct(st_in.shape, F32)],
        out_specs=[blk, stspec],
        scratch=scratch)


def _ssd_kernel(*refs, reverse, post, tb):
    if post:
        (xs_ref, dt_ref, z_ref, of_ref, dtb_ref, a_ref, e_ref, dsk_ref, nrm_ref,
         st_in_ref, o_ref, st_ref, o_scr) = refs
    else:
        (xs_ref, dt_ref, dtb_ref, a_ref, e_ref, st_in_ref, o_ref, st_ref) = refs

    gn = M2_GROUPS * M2_STATE
    ri = lax.broadcasted_iota(jnp.int32, (CHUNK, BR_W), 0)
    cj = lax.broadcasted_iota(jnp.int32, (CHUNK, BR_W), 1) % CHUNK
    mask = (cj >= ri) if reverse else (cj <= ri)
    teye = (cj == ri).astype(F32)
    r8 = lax.broadcasted_iota(jnp.int32, (BR_W, BR_W), 0) // M2_HEADDIM
    c8 = lax.broadcasted_iota(jnp.int32, (BR_W, BR_W), 1) // M2_HEADDIM
    bdmask = r8 == c8
    hpg = M2_HEADS // M2_GROUPS
    g_row = lax.broadcasted_iota(jnp.int32, (BR_W, gn), 0) // (hpg * CHUNK)
    g_col = lax.broadcasted_iota(jnp.int32, (BR_W, gn), 1) // M2_STATE
    bmask = g_row == g_col
    s_row = lax.broadcasted_iota(jnp.int32, (gn, BR_W), 0) // M2_STATE
    s_col = lax.broadcasted_iota(jnp.int32, (gn, BR_W), 1) // (hpg * M2_HEADDIM)
    smask = s_row == s_col
    zero = jnp.zeros((), BF16)

    nchunk = tb // CHUNK
    order = range(nchunk - 1, -1, -1) if reverse else range(nchunk)
    for bb in range(BPS):
        xo = bb * M2_XBC
        osl = slice(bb * BR_W, (bb + 1) * BR_W)
        dt = _softplus(dt_ref[:, bb * DT_PAD:(bb + 1) * DT_PAD] + dtb_ref[...])
        cum_n = _chunk_cumsum(dt * a_ref[...], reverse)
        ex = _dot(jnp.concatenate(_split3(cum_n) + _split3(dt), axis=0), e_ref[...])
        cum_all = _sum3(ex[0:3 * tb], tb, 0)
        dt_all = _sum3(ex[3 * tb:6 * tb], tb, 0)
        for c in order:
            rs = slice(c * CHUNK, (c + 1) * CHUNK)
            xh = xs_ref[rs, xo:xo + BR_W]
            bm = xs_ref[rs, xo + BR_W:xo + BR_W + gn]
            cm = xs_ref[rs, xo + BR_W + gn:xo + BR_W + 2 * gn]
            cum = cum_all[rs, :]
            dt_e = dt_all[rs, :]
            cum_end = cum[0:1, :] if reverse else cum[CHUNK - 1:CHUNK, :]
            cum_row = jnp.sum(cum * teye, axis=0, keepdims=True)
            dt_row = jnp.sum(dt_e * teye, axis=0, keepdims=True)
            decay = jnp.exp(jnp.where(mask, cum - cum_row, -jnp.inf))
            bm8 = jnp.where(bmask, jnp.concatenate([bm] * M2_HEADS, axis=0), zero)
            scores = _dot_nt(cm, bm8)
            w = (scores * decay * dt_row).astype(BF16)
            xbd = jnp.where(bdmask, jnp.concatenate([xh] * M2_HEADS, axis=0), zero)
            st = st_ref[bb]
            y = _dot(w, xbd) + _dot(cm, st.astype(BF16)) * jnp.exp(cum)
            wx = (xh.astype(F32) * (jnp.exp(cum_end - cum) * dt_e)).astype(BF16)
            st_ref[bb] = jnp.exp(cum_end) * st + jnp.where(smask, _dot_tn(bm, wx), 0.0)
            if post:
                o_scr[rs, osl] = y
            else:
                o_ref[rs, osl] = y
            yield
        if post:
            y = of_ref[:, osl] + o_scr[:, osl] + dsk_ref[...] * xs_ref[:, xo:xo + BR_W].astype(F32)
            yz = y * _silu(z_ref[:, osl])
            ms = jnp.mean(yz * yz, axis=-1, keepdims=True)
            o_ref[:, osl] = (yz * lax.rsqrt(ms + EPS)) * nrm_ref[...]


def _ssd_part(uxs, udt, uz, of, st_in, dtb, a, e, dsk, nrm, reverse):
    tp = uxs.shape[0]
    tb = min(256, tp)
    nblk = tp // tb
    post = of is not None
    gn = M2_GROUPS * M2_STATE
    tix = (lambda i: nblk - 1 - i) if reverse else (lambda i: i)
    tile = lambda w: pl.BlockSpec((tb, BPS * w), lambda b, i: (tix(i), b))
    c2 = lambda b, i: (0, 0)
    stspec = pl.BlockSpec((BPS, gn, BR_W), lambda b, i: (b, 0, 0))
    in_specs = [tile(M2_XBC), tile(DT_PAD)]
    args = [uxs, udt]
    scratch = []
    if post:
        in_specs += [tile(BR_W), tile(BR_W)]
        args += [uz, of]
        scratch = [pltpu.VMEM((tb, BPS * BR_W), F32)]
    in_specs += [pl.BlockSpec((1, DT_PAD), c2), pl.BlockSpec((1, DT_PAD), c2),
                 pl.BlockSpec((DT_PAD, BR_W), c2)]
    args += [dtb, a, e]
    if post:
        in_specs += [pl.BlockSpec((1, BR_W), c2), pl.BlockSpec((1, BR_W), c2)]
        args += [dsk, nrm]
    in_specs.append(stspec)
    args.append(st_in)
    return dict(
        kern=functools.partial(_ssd_kernel, reverse=reverse, post=post, tb=tb),
        in_specs=in_specs, args=args, st_in=[a is st_in for a in args].index(True),
        out_shape=[jax.ShapeDtypeStruct((tp, NB * BR_W), F32), jax.ShapeDtypeStruct(st_in.shape, F32)],
        out_specs=[tile(BR_W), stspec],
        scratch=scratch)


def _block_diag(blocks):
    n, a, b = blocks.shape
    eye = jnp.eye(n, dtype=blocks.dtype)
    return jnp.einsum('nab,nm->namb', blocks, eye).reshape(n * a, n * b)


def _s5_params(a_re, a_im, log_step, b_re, b_im, c_re, c_im):
    g, n, p = b_re.shape
    gpc = g // S5_NCH
    ar, ai = a_re.astype(F32), a_im.astype(F32)
    step = jnp.exp(log_step.astype(F32))[..., None]
    mag = jnp.exp(ar * step)
    ab_re, ab_im = mag * jnp.cos(ai * step), mag * jnp.sin(ai * step)
    den = ar * ar + ai * ai
    co_re = ((ab_re - 1.0) * ar + ab_im * ai) / den
    co_im = (ab_im * ar - (ab_re - 1.0) * ai) / den

    def chunked(m):
        return jax.vmap(_block_diag)(m.reshape(S5_NCH, gpc, m.shape[1], m.shape[2]))

    br, bi = b_re.astype(F32), b_im.astype(F32)
    cr, ci = c_re.astype(F32), c_im.astype(F32)
    sw = lambda m: jnp.swapaxes(m, 1, 2)
    m = S5_BLK
    out = []
    for d in range(2):
        pw = [(jnp.ones_like(ab_re[d]), jnp.zeros_like(ab_im[d]))]
        for _ in range(m):
            pr, pi = pw[-1]
            pw.append((pr * ab_re[d] - pi * ab_im[d], pr * ab_im[d] + pi * ab_re[d]))
        ab = jnp.concatenate([pw[m][0].reshape(S5_NCH, gpc * n), pw[m][1].reshape(S5_NCH, gpc * n)], axis=-1)
        ab = jnp.broadcast_to(ab[:, None, :], (S5_NCH, NB, 2 * gpc * n))
        cc_re = cr * co_re[d][:, None, :] - ci * co_im[d][:, None, :]
        cc_im = cr * co_im[d][:, None, :] + ci * co_re[d][:, None, :]
        wd_rows, wy_cols, wf_rows = [], [], []
        for i in range(m):
            e = i if d else m - 1 - i
            gdist = m - i if d else i + 1
            pr, pi = pw[e][0][:, :, None], pw[e][1][:, :, None]
            wd_rows.append(jnp.concatenate([chunked(sw(pr * br - pi * bi)), chunked(sw(pr * bi + pi * br))],
                                           axis=-1))
            qr, qi = pw[gdist][0][:, None, :], pw[gdist][1][:, None, :]
            wy_cols.append(jnp.concatenate([chunked(sw(cc_re * qr - cc_im * qi)),
                                            chunked(sw(-(cc_re * qi + cc_im * qr)))], axis=1))
            blocks = []
            for j in range(m):
                if (i >= j) if d else (i <= j):
                    tr, ti = pw[abs(j - i)][0][:, None, :], pw[abs(j - i)][1][:, None, :]
                    car, cai = cc_re * tr - cc_im * ti, cc_re * ti + cc_im * tr
                    thru = jnp.einsum('gon,gni->gio', car, br) - jnp.einsum('gon,gni->gio', cai, bi)
                    blocks.append(chunked(thru))
                else:
                    blocks.append(jnp.zeros((S5_NCH, gpc * p, gpc * p), F32))
            wf_rows.append(jnp.concatenate(blocks, axis=-1))
        out.append((jnp.concatenate(wd_rows, axis=1).astype(BF16), ab,
                    jnp.concatenate(wy_cols, axis=-1).astype(BF16),
                    jnp.concatenate(wf_rows, axis=1).astype(BF16)))
    return out


def _lane_vec(v, d):
    out = jnp.zeros((1, DT_PAD), F32)
    return lax.dynamic_update_slice(out, v.astype(F32)[None, :], (0, M2_HEADS * d))


def _head_expand(d):
    r = jnp.arange(DT_PAD)[:, None]
    c = jnp.arange(BR_W)[None, :] // M2_HEADDIM
    return (r == c + M2_HEADS * d).astype(BF16)


def kernel(x, c, ctx, c_ctx, norm_w, w_mod, b_mod, w_in, hg_lb_logits, hg_norm, s5_a_re, s5_a_im, s5_log_step, s5_b_re, s5_b_im, s5_c_re, s5_c_im, s5_d, s5_w_glu, s5_b_glu, lru_conv_w, lru_conv_b, lru_gate_w, lru_gate_b, lru_lam, m2_conv_w, m2_conv_b, m2_dt_bias, m2_a_log, m2_d, m2_norm, w_branch, w_gate, b_gate, w_out, final_norm):
    bsz, seq, dm = x.shape
    n_ctx = ctx.shape[1]
    depth = norm_w.shape[0]
    assert bsz == NB and seq % 256 == 0 and n_ctx % CHUNK == 0

    xl, xc = x, ctx
    c_all = jnp.concatenate([c, jnp.broadcast_to(c_ctx[None, :], (NB, dm))], axis=0)
    lb_all = jnp.cumsum(jax.nn.softmax(hg_lb_logits.astype(F32), axis=0), axis=0)
    perm = _tile_perm()
    perm_t = perm.T

    o_s5 = 5 * BR_W
    o_lru = o_s5 + 2 * BR_W
    o_lz = o_lru + BR_W
    o_m2 = o_lru + 2 * BR_W
    o_dt = o_m2 + M2_XBC
    o_mz = o_dt + 2 * M2_HEADS
    gn = M2_GROUPS * M2_STATE

    for l in range(depth):
        mod = _mod_call(c_all, w_mod[l].astype(BF16), b_mod[l][None, :])
        nw = norm_w[l][None, :]
        wl = w_in[l]
        w_s5 = wl[:, o_s5:o_lru].astype(BF16)
        w_lx = wl[:, o_lru:o_lz].astype(BF16)
        w_lz = wl[:, o_lz:o_m2].astype(BF16)
        w_mx = wl[:, o_m2:o_dt].astype(BF16)
        w_bm = jnp.concatenate(
            [wl[:, 0:o_s5], jnp.pad(wl[:, o_dt:o_mz], ((0, 0), (0, DT_PAD - 2 * M2_HEADS))), wl[:, o_mz:]],
            axis=1).astype(BF16)
        wg = w_gate[l].astype(BF16)
        bg = b_gate[l][:, None, :]
        wb = w_branch[l].astype(BF16)
        wo = w_out[l].astype(BF16)

        s5_w = _s5_params(s5_a_re[l], s5_a_im[l], s5_log_step[l], s5_b_re[l], s5_b_im[l],
                          s5_c_re[l], s5_c_im[l])
        nbh = lru_gate_w.shape[3] // LRU_HALVES
        lru_wg = [jnp.stack([jnp.concatenate(
            [_block_diag(0.5 * lru_gate_w[l, d, g, j * nbh:(j + 1) * nbh]) for g in range(2)], axis=1)
            for j in range(LRU_HALVES)]).astype(BF16) for d in range(2)]
        lru_bg = [jnp.stack([jnp.concatenate(
            [0.5 * lru_gate_b[l, d, g, j * nbh:(j + 1) * nbh].reshape(1, -1) for g in range(2)], axis=1)
            for j in range(LRU_HALVES)]) for d in range(2)]

        parts = []
        for name, x3, row0, period in (("ctx", xc, NB, n_ctx), ("lat", xl, 0, CHUNK)):
            sh = mod[row0:row0 + NB, 0:dm]
            sc = mod[row0:row0 + NB, dm:2 * dm]
            gt = mod[row0:row0 + NB, 2 * dm:3 * dm]
            ub, ucx, ucz, uxs, uq, uv, uff, ufb, uaz, udt, udz = _proj_call(
                x3, nw, sc, sh, perm, perm_t, w_s5, w_lx, w_lz, lru_conv_w[l], lru_conv_b[l][None, :],
                w_mx, w_bm, (BR_W,) * 5 + (DT_PAD, BR_W), m2_conv_w[l], m2_conv_b[l][None, :],
                period // CHUNK, "proj_" + name)
            parts.append(dict(name=name, x=x3, sc=sc, sh=sh, gt=gt, uq=uq, uv=uv, uf=(uff, ufb), uaz=uaz,
                              ub=ub, ucx=ucx, ucz=ucz, uxs=uxs, udt=udt, udz=udz))

        ys = [dict(), dict()]
        for d, reverse in ((0, False), (1, True)):
            tag = "bwd" if reverse else "fwd"
            st_a = jnp.zeros((NB, HG_HEADS, HG_DK, HG_DK), F32)
            st_b = jnp.zeros((NB, 2 * (BR_W // S5_GROUP) * S5_STATE), F32)
            st_c = jnp.zeros((NB, BR_W), F32)
            st_d = jnp.zeros((NB, gn, BR_W), F32)
            for pi, p in enumerate(parts):
                nm = tag + "_" + p["name"]
                of = ys[pi] if reverse else dict(a=None, b=None, c=None, d=None)
                r = p["ub"].shape[0]
                tp = r // NB
                (ob, st_b), (oc, st_c) = _fused_call(
                    [_s5_part(p["ub"], of["b"], st_b, *s5_w[d], s5_d[l][None, :],
                              s5_w_glu[l].astype(BF16), s5_b_glu[l][None, :], reverse),
                     _lru_part(p["ucx"], p["ucz"], of["c"], st_c, lru_wg[d], lru_bg[d],
                               lru_lam[l, d][None, :], reverse)],
                    (r // TILE,), "tm_" + nm)
                (oa, st_a), (od, st_d) = _fused_call(
                    [_hgrn_part(p["uq"], p["uv"], p["uf"][d], p["uaz"], of["a"], st_a,
                                lb_all[l, d][None, :], hg_norm[l][None, :], reverse),
                     _ssd_part(p["uxs"], p["udt"], p["udz"], of["d"], st_d,
                               _lane_vec(m2_dt_bias[l, d], d),
                               _lane_vec(-jnp.exp(m2_a_log[l, d].astype(F32)), d), _head_expand(d),
                               jnp.repeat(m2_d[l].astype(F32), M2_HEADDIM)[None, :],
                               m2_norm[l][None, :], reverse)],
                    (NB // BPS, tp // min(256, tp)), "bm_" + nm)
                ys[pi] = dict(a=oa, b=ob, c=oc, d=od)

        last = l == depth - 1
        for pi, p in enumerate(parts):
            if last and p["name"] == "ctx":
                continue
            y = ys[pi]
            out = _merge_call(p["x"], (y["a"], y["b"], y["c"], y["d"]), nw, p["sc"], p["sh"], p["gt"],
                              perm, wg, bg, wb, wo, final_norm[None, :], last, "merge_" + p["name"])
            if p["name"] == "ctx":
                xc = out
            else:
                xl = out
    return xl
```

```python
import functools
import math

import jax
import jax.numpy as jnp
import numpy as np
from jax import lax
from jax.experimental import pallas as pl
from jax.experimental.pallas import tpu as pltpu

F32 = jnp.float32
BF16 = jnp.bfloat16

NB = 8
EPS = 1e-6
CHUNK = 64
TILE = CHUNK * NB
HALO = 2 * NB
CONV_W = 4
BR_W = 512
HG_HEADS = 4
HG_DK = BR_W // HG_HEADS
S5_GROUP = 16
S5_STATE = 64
S5_NCH = 4
S5_BLK = 4
LRU_C = 8.0
LRU_HALVES = 2
M2_HEADDIM = 64
M2_HEADS = BR_W // M2_HEADDIM
M2_GROUPS = 2
M2_STATE = 64
M2_XBC = BR_W + 2 * M2_GROUPS * M2_STATE
DT_PAD = 128
BPS = 4
VMEM_LIMIT = 56 * 1024 * 1024


def _cp(n_axes):
    return pltpu.CompilerParams(dimension_semantics=("arbitrary",) * n_axes,
                                vmem_limit_bytes=VMEM_LIMIT)


def _const_spec(shape):
    nd = len(shape)
    return pl.BlockSpec(shape, lambda *_: (0,) * nd, pipeline_mode=pl.Buffered(1))


def _dot(a, b):
    return jnp.dot(a, b, preferred_element_type=F32)


def _dot_nt(a, b):
    return lax.dot_general(a, b, (((1,), (1,)), ((), ())), preferred_element_type=F32)


def _dot_tn(a, b):
    return lax.dot_general(a, b, (((0,), (0,)), ((), ())), preferred_element_type=F32)


def _split3(x):
    hi = x.astype(BF16)
    r1 = x - hi.astype(F32)
    mid = r1.astype(BF16)
    lo = (r1 - mid.astype(F32)).astype(BF16)
    return hi, mid, lo


def _sum3(y, n, axis):
    if axis == 1:
        return (y[:, 0:n] + y[:, n:2 * n]) + y[:, 2 * n:3 * n]
    return (y[0:n] + y[n:2 * n]) + y[2 * n:3 * n]


def _chunk_cumsum(x, reverse):
    n = x.shape[0]
    i = lax.broadcasted_iota(jnp.int32, (n, n), 0)
    j = lax.broadcasted_iota(jnp.int32, (n, n), 1)
    tri = ((i // CHUNK == j // CHUNK) & ((j >= i) if reverse else (j <= i))).astype(BF16)
    y = _dot(tri, jnp.concatenate(_split3(x), axis=1))
    return _sum3(y, x.shape[1], 1)


def _sigmoid(x):
    return 0.5 * jnp.tanh(0.5 * x) + 0.5


def _silu(x):
    return x * _sigmoid(x)


def _softplus(x):
    return jnp.maximum(x, 0.0) + jnp.log1p(jnp.exp(-jnp.abs(x)))


def _adaln(x3, nw, sc, sh):
    nb, n, d = x3.shape
    ms = jnp.mean(x3 * x3, axis=-1, keepdims=True)
    xn = (x3 * lax.rsqrt(ms + EPS)) * nw
    h = xn * (1.0 + sc)[:, None, :] + sh[:, None, :]
    return h.reshape(nb * n, d)


def _tile_perm():
    i = jnp.arange(TILE)
    src = (i % CHUNK) * NB + i // CHUNK
    return (src[:, None] == jnp.arange(TILE)[None, :]).astype(BF16)


def _halo_select(first_step):
    i = jnp.arange(HALO)
    src = (i % NB) * NB + first_step + i // NB
    return (src[:, None] == jnp.arange(NB * NB)[None, :]).astype(BF16)


def _mod_kernel(c_ref, w_ref, b_ref, o_ref):
    s = _silu(c_ref[...]).astype(BF16)
    o_ref[...] = _dot(s, w_ref[...]) + b_ref[...]


def _mod_call(c_all, w, b):
    n, d = c_all.shape
    m = w.shape[1]
    return pl.pallas_call(
        _mod_kernel,
        out_shape=jax.ShapeDtypeStruct((n, m), F32),
        grid=(1,),
        in_specs=[pl.BlockSpec((n, d), lambda i: (0, 0)),
                  pl.BlockSpec((d, m), lambda i: (0, 0)),
                  pl.BlockSpec((1, m), lambda i: (0, 0))],
        out_specs=pl.BlockSpec((n, m), lambda i: (0, 0)),
        compiler_params=_cp(1),
        name="mod",
    )(c_all, w, b)


def _conv_tm(hb, halo, w_ref, cw_ref, cb_ref, ptiles):
    u = _dot(hb, w_ref[...])
    width = u.shape[1]
    if ptiles == 1:
        up = jnp.zeros((NB, width), F32)
        un = jnp.zeros((HALO, width), F32)
    else:
        i = pl.program_id(0)
        vp = (i % ptiles != 0).astype(F32)
        vn = (i % ptiles != ptiles - 1).astype(F32)
        up = _dot(halo[0], w_ref[...])[NB:HALO, :] * vp
        un = _dot(halo[1], w_ref[...]) * vn
    ue = jnp.concatenate([up, u, un], axis=0)
    xc = cb_ref[...] + cw_ref[0:1, :] * ue[0:TILE, :]
    for k in range(1, CONV_W):
        xc = xc + cw_ref[k:k + 1, :] * ue[k * NB:k * NB + TILE, :]
    return xc


def _store_bm(o_ref, u, wd, c0):
    cw = u.shape[1]
    for b in range(NB):
        o_ref[:, b * wd + c0:b * wd + c0 + cw] = u[b * CHUNK:(b + 1) * CHUNK, :].astype(o_ref.dtype)


def _proj_kernel(*refs, ptiles, widths):
    if ptiles == 1:
        x_ref, nw_ref, sc_ref, sh_ref, p_ref, pt_ref = refs[:6]
        rest = refs[6:]
        halo = None
    else:
        x_ref, xp_ref, xn_ref, nw_ref, sc_ref, sh_ref, p_ref, pt_ref, sp_ref, sn_ref = refs[:10]
        rest = refs[10:]
        halo = tuple(
            _dot(s_ref[...], _adaln(r[...], nw_ref[...], sc_ref[...], sh_ref[...]).astype(BF16)).astype(BF16)
            for s_ref, r in ((sp_ref, xp_ref), (sn_ref, xn_ref)))
    (w_s5_ref, w_lx_ref, w_lz_ref, cwl_ref, cbl_ref, w_x_ref, w_ref, cwm_ref, cbm_ref,
     ub_ref, ucx_ref, ucz_ref, uxs_ref) = rest[:13]
    o_refs = rest[13:]
    hb_bm = _adaln(x_ref[...], nw_ref[...], sc_ref[...], sh_ref[...]).astype(BF16)
    hb_tm = _dot(pt_ref[...], hb_bm).astype(BF16)
    ub_ref[...] = _dot(hb_tm, w_s5_ref[...])
    ucx_ref[...] = _conv_tm(hb_tm, halo, w_lx_ref, cwl_ref, cbl_ref, ptiles)
    ucz_ref[...] = _dot(hb_tm, w_lz_ref[...])
    xs = _silu(_conv_tm(hb_tm, halo, w_x_ref, cwm_ref, cbm_ref, ptiles)).astype(BF16)
    _store_bm(uxs_ref, _dot(p_ref[...], xs), M2_XBC, 0)
    off = 0
    for o_ref, wd in zip(o_refs, widths):
        for c0 in range(0, wd, BR_W):
            cw = min(BR_W, wd - c0)
            _store_bm(o_ref, _dot(hb_bm, w_ref[:, off + c0:off + c0 + cw]), wd, c0)
        off += wd


def _proj_call(x3, nw, sc, sh, perm, perm_t, w_s5, w_lx, w_lz, cwl, cbl, w_x, w, widths, cwm, cbm,
               ptiles, name):
    nb, t, d = x3.shape
    r = nb * t
    blk = lambda i: (0, i, 0)
    row = lambda i: (i, 0)
    in_specs = [pl.BlockSpec((NB, CHUNK, d), blk)]
    args = [x3]
    if ptiles > 1:
        per = CHUNK // NB
        last = t // NB - 1
        in_specs += [pl.BlockSpec((NB, NB, d), lambda i: (0, jnp.maximum(i * per - 1, 0), 0)),
                     pl.BlockSpec((NB, NB, d), lambda i: (0, jnp.minimum((i + 1) * per, last), 0))]
        args += [x3, x3]
    consts = [nw, sc, sh, perm, perm_t]
    if ptiles > 1:
        consts += [_halo_select(NB - 2), _halo_select(0)]
    consts += [w_s5, w_lx, w_lz, cwl, cbl, w_x, w, cwm, cbm]
    for a in consts:
        in_specs.append(_const_spec(a.shape))
        args.append(a)
    tm_w = (2 * BR_W, BR_W, BR_W)
    bm = [(M2_XBC, BF16)] + [(wd, F32) for wd in widths]
    return pl.pallas_call(
        functools.partial(_proj_kernel, ptiles=ptiles, widths=widths),
        out_shape=[jax.ShapeDtypeStruct((r, wd), F32) for wd in tm_w]
                  + [jax.ShapeDtypeStruct((t, NB * wd), dt) for wd, dt in bm],
        grid=(t // CHUNK,),
        in_specs=in_specs,
        out_specs=[pl.BlockSpec((TILE, wd), row) for wd in tm_w]
                  + [pl.BlockSpec((CHUNK, NB * wd), row) for wd, _ in bm],
        compiler_params=_cp(1),
        name=name,
    )(*args)


def _merge_kernel(x_ref, ya_ref, yb_ref, yc_ref, yd_ref, nw_ref, sc_ref, sh_ref, gt_ref, p_ref,
                  wg_ref, bg_ref, wb_ref, wo_ref, fn_ref, o_ref, *, final):
    x3 = x_ref[...]
    hb = _adaln(x3, nw_ref[...], sc_ref[...], sh_ref[...]).astype(BF16)

    def from_bm(y_ref):
        return jnp.concatenate([y_ref[:, b * BR_W:(b + 1) * BR_W] for b in range(NB)],
                               axis=0).astype(BF16)

    def from_tm(y_ref):
        return _dot(p_ref[...], y_ref[...].astype(BF16)).astype(BF16)

    ys = (from_bm(ya_ref), from_tm(yb_ref), from_tm(yc_ref), from_bm(yd_ref))
    m = None
    for k, y in enumerate(ys):
        g = _sigmoid(_dot(hb, wg_ref[k]) + bg_ref[k])
        p = _dot(y, wb_ref[k])
        m = g * p if m is None else m + g * p
    upd = _dot(m.astype(BF16), wo_ref[...])
    out = x3 + upd.reshape(x3.shape) * gt_ref[...][:, None, :]
    if final:
        ms = jnp.mean(out * out, axis=-1, keepdims=True)
        out = (out * lax.rsqrt(ms + EPS)) * fn_ref[...]
    o_ref[...] = out


def _merge_call(x3, ys, nw, sc, sh, gt, perm, wg, bg, wb, wo, fn, final, name):
    nb, t, d = x3.shape
    blk = lambda i: (0, i, 0)
    row = lambda i: (i, 0)
    tm_spec = pl.BlockSpec((TILE, BR_W), row)
    bm_spec = pl.BlockSpec((CHUNK, NB * BR_W), row)
    return pl.pallas_call(
        functools.partial(_merge_kernel, final=final),
        out_shape=jax.ShapeDtypeStruct(x3.shape, F32),
        grid=(t // CHUNK,),
        in_specs=[pl.BlockSpec((NB, CHUNK, d), blk), bm_spec, tm_spec, tm_spec, bm_spec,
                  _const_spec((1, d)), _const_spec((NB, d)), _const_spec((NB, d)),
                  _const_spec((NB, d)), _const_spec((TILE, TILE)),
                  _const_spec(wg.shape), _const_spec(bg.shape), _const_spec(wb.shape),
                  _const_spec(wo.shape), _const_spec((1, d))],
        out_specs=pl.BlockSpec((NB, CHUNK, d), blk),
        compiler_params=_cp(1),
        name=name,
    )(x3, *ys, nw, sc, sh, gt, perm, wg, bg, wb, wo, fn)


def _fused_call(parts, grid, name):
    n_in = [len(p["args"]) for p in parts]
    n_out = [len(p["out_shape"]) for p in parts]
    n_scr = [len(p["scratch"]) for p in parts]

    def kern(*refs):
        ins = refs[:sum(n_in)]
        outs = refs[sum(n_in):sum(n_in) + sum(n_out)]
        scr = refs[sum(n_in) + sum(n_out):]
        split = []
        i = o = s = 0
        for a, b, c in zip(n_in, n_out, n_scr):
            split.append((ins[i:i + a], outs[o:o + b], scr[s:s + c]))
            i, o, s = i + a, o + b, s + c

        @pl.when(pl.program_id(len(grid) - 1) == 0)
        def _():
            for p, (pin, pout, _) in zip(parts, split):
                pout[1][...] = pin[p["st_in"]][...]

        active = [p["kern"](*pin, *pout, *pscr) for p, (pin, pout, pscr) in zip(parts, split)]
        while active:
            for g in list(active):
                if next(g, StopIteration) is StopIteration:
                    active.remove(g)

    res = pl.pallas_call(
        kern,
        out_shape=[x for p in parts for x in p["out_shape"]],
        grid=grid,
        in_specs=[x for p in parts for x in p["in_specs"]],
        out_specs=[x for p in parts for x in p["out_specs"]],
        scratch_shapes=[x for p in parts for x in p["scratch"]],
        compiler_params=_cp(len(grid)),
        name=name,
    )(*[x for p in parts for x in p["args"]])
    out, o = [], 0
    for b in n_out:
        out.append(res[o:o + b])
        o += b
    return out


def _lru_kernel(*refs, reverse, post):
    if post:
        (x_ref, z_ref, of_ref, st_in_ref, wg_ref, bg_ref, lam_ref, o_ref, st_ref, a_scr, b_scr) = refs
    else:
        (x_ref, st_in_ref, wg_ref, bg_ref, lam_ref, o_ref, st_ref, a_scr, b_scr) = refs

    c1 = (-0.5 * LRU_C) * _softplus(-lam_ref[...])
    nq = 4
    qr = TILE // nq
    hw = BR_W // LRU_HALVES
    for q in (range(nq - 1, -1, -1) if reverse else range(nq)):
        rs = slice(q * qr, (q + 1) * qr)
        for j in range(LRU_HALVES):
            cs = slice(j * hw, (j + 1) * hw)
            xc = x_ref[rs, cs]
            th = jnp.tanh(_dot(xc.astype(BF16), wg_ref[j]) + bg_ref[j])
            log_a = c1[:, cs] * th[:, 0:hw] + c1[:, cs]
            a = jnp.exp(log_a)
            a_scr[rs, cs] = a
            v = -jnp.tanh(log_a) * (a * a + 1.0)
            root = jnp.where(v > 0.0, v * lax.rsqrt(v), 0.0)
            b_scr[rs, cs] = root * ((0.5 * th[:, hw:2 * hw] + 0.5) * xc)
        yield
    h = st_ref[...]
    for i in range(CHUNK):
        r0 = ((CHUNK - 1 - i) if reverse else i) * NB
        h = a_scr[r0:r0 + NB, :] * h + b_scr[r0:r0 + NB, :]
        b_scr[r0:r0 + NB, :] = h
        yield
    st_ref[...] = h
    if post:
        o_ref[...] = (of_ref[...] + b_scr[...]) * _silu(z_ref[...])
    else:
        o_ref[...] = b_scr[...]


def _pick_spec(a, lead):
    k = len(lead)
    return pl.BlockSpec((None,) * k + a.shape[k:], lambda *_: tuple(lead) + (0,) * (a.ndim - k))


def _lru_part(xc, z, of, st_in, wg, bg, ld, lam, reverse):
    r = xc.shape[0]
    nblk = r // TILE
    post = of is not None
    tmap = (lambda i: (nblk - 1 - i, 0)) if reverse else (lambda i: (i, 0))
    c2 = lambda i: (0, 0)
    tile = pl.BlockSpec((TILE, BR_W), tmap)
    in_specs = [tile]
    args = [xc]
    if post:
        in_specs += [tile, tile]
        args += [z, of]
    in_specs += [pl.BlockSpec((NB, BR_W), c2), _pick_spec(wg, ld), _pick_spec(bg, ld),
                 pl.BlockSpec((1, BR_W), c2)]
    args += [st_in, wg, bg, lam]
    return dict(
        kern=functools.partial(_lru_kernel, reverse=reverse, post=post),
        in_specs=in_specs, args=args, st_in=[a is st_in for a in args].index(True),
        out_shape=[jax.ShapeDtypeStruct((r, BR_W), F32), jax.ShapeDtypeStruct((NB, BR_W), F32)],
        out_specs=[tile, pl.BlockSpec((NB, BR_W), c2)],
        scratch=[pltpu.VMEM((TILE, BR_W), F32), pltpu.VMEM((TILE, BR_W), F32)])


def _gelu_tanh(x):
    return 0.5 * x * (1.0 + jnp.tanh(math.sqrt(2.0 / math.pi) * (x + 0.044715 * (x * x * x))))


def _s5_kernel(*refs, reverse, post):
    if post:
        (u_ref, of_ref, st_in_ref, wd_ref, ab_ref, wy_ref, wf_ref, dsk_ref, wglu_ref, bglu_ref,
         o_ref, st_ref, s_scr, y_scr) = refs
    else:
        (u_ref, st_in_ref, wd_ref, ab_ref, wy_ref, wf_ref, o_ref, st_ref, s_scr) = refs
        y_scr = o_ref

    half = (BR_W // S5_GROUP) * S5_STATE // S5_NCH
    cin = BR_W // S5_NCH
    nk = CHUNK // S5_BLK
    u4s = []
    for c in range(S5_NCH):
        ug = u_ref[:, c * cin:(c + 1) * cin].reshape(nk, S5_BLK, NB, cin)
        u4 = jnp.concatenate([ug[:, i].reshape(nk * NB, cin) for i in range(S5_BLK)], axis=1).astype(BF16)
        u4s.append(u4)
        s_scr[c] = _dot(u4, wd_ref[c])
        yield
    for c in range(S5_NCH):
        a_re = ab_ref[c, :, 0:half]
        a_im = ab_ref[c, :, half:2 * half]
        c0 = c * 2 * half
        s_re = st_ref[:, c0:c0 + half]
        s_im = st_ref[:, c0 + half:c0 + 2 * half]
        for kk in range(nk):
            r0 = ((nk - 1 - kk) if reverse else kk) * NB
            n_re = a_re * s_re - a_im * s_im + s_scr[c, r0:r0 + NB, 0:half]
            n_im = a_re * s_im + a_im * s_re + s_scr[c, r0:r0 + NB, half:2 * half]
            s_scr[c, r0:r0 + NB, 0:half] = s_re
            s_scr[c, r0:r0 + NB, half:2 * half] = s_im
            s_re, s_im = n_re, n_im
            yield
        st_ref[:, c0:c0 + half] = s_re
        st_ref[:, c0 + half:c0 + 2 * half] = s_im
        y4 = _dot(s_scr[c].astype(BF16), wy_ref[c]) + _dot(u4s[c], wf_ref[c])
        y = jnp.stack([y4[:, i * cin:(i + 1) * cin].reshape(nk, NB, cin) for i in range(S5_BLK)], axis=1)
        y_scr[:, c * cin:(c + 1) * cin] = y.reshape(TILE, cin)
        yield
    if post:
        u = u_ref[:, 0:BR_W]
        z = u_ref[:, BR_W:2 * BR_W]
        y = of_ref[...] + y_scr[...] + dsk_ref[...] * u
        g = _gelu_tanh(y)
        gl = _dot(g.astype(BF16), wglu_ref[...]) + bglu_ref[...]
        o_ref[...] = g * _sigmoid(gl) * _silu(z)


def _s5_part(u, of, st_in, wd, ab, wy, wf, ld, dsk, wglu, bglu, reverse):
    r = u.shape[0]
    nblk = r // TILE
    post = of is not None
    nst = st_in.shape[1]
    tmap = (lambda i: (nblk - 1 - i, 0)) if reverse else (lambda i: (i, 0))
    c2 = lambda i: (0, 0)
    in_specs = [pl.BlockSpec((TILE, 2 * BR_W), tmap)]
    args = [u]
    if post:
        in_specs.append(pl.BlockSpec((TILE, BR_W), tmap))
        args.append(of)
    in_specs += [pl.BlockSpec((NB, nst), c2)] + [_pick_spec(a, ld) for a in (wd, ab, wy, wf)]
    args += [st_in, wd, ab, wy, wf]
    scratch = [pltpu.VMEM((S5_NCH, TILE // S5_BLK, nst // S5_NCH), F32)]
    if post:
        in_specs += [pl.BlockSpec((1, BR_W), c2), pl.BlockSpec((BR_W, BR_W), c2),
                     pl.BlockSpec((1, BR_W), c2)]
        args += [dsk, wglu, bglu]
        scratch.append(pltpu.VMEM((TILE, BR_W), F32))
    return dict(
        kern=functools.partial(_s5_kernel, reverse=reverse, post=post),
        in_specs=in_specs, args=args, st_in=[a is st_in for a in args].index(True),
        out_shape=[jax.ShapeDtypeStruct((r, BR_W), F32), jax.ShapeDtypeStruct((NB, nst), F32)],
        out_specs=[pl.BlockSpec((TILE, BR_W), tmap), pl.BlockSpec((NB, nst), c2)],
        scratch=scratch)


def _tri_mask(n, reverse):
    i = lax.broadcasted_iota(jnp.int32, (n, n), 0)
    j = lax.broadcasted_iota(jnp.int32, (n, n), 1)
    return (j >= i) if reverse else (j <= i)


def _hgrn_kernel(*refs, reverse, post, tb):
    if post:
        (q_ref, v_ref, f_ref, z_ref, of_ref, lb_ref, nrm_ref, st_in_ref,
         o_ref, st_ref, o_scr) = refs
    else:
        (q_ref, v_ref, f_ref, lb_ref, st_in_ref, o_ref, st_ref) = refs

    mask = _tri_mask(CHUNK, reverse)
    lb = lb_ref[...]
    nchunk = tb // CHUNK
    order = range(nchunk - 1, -1, -1) if reverse else range(nchunk)
    for bb in range(BPS):
        co = bb * BR_W
        f_all = lb + (1.0 - lb) * _sigmoid(f_ref[:, co:co + BR_W])
        b_all = _chunk_cumsum(jnp.log(f_all), reverse)
        for c in order:
            rs = slice(c * CHUNK, (c + 1) * CHUNK)
            k = 1.0 - f_all[rs, :]
            qs = _silu(q_ref[rs, co:co + BR_W])
            b = b_all[rs, :]
            b_end = b[0:1, :] if reverse else b[CHUNK - 1:CHUNK, :]
            mid = 0.5 * b_end
            qt = (qs * jnp.exp(b - mid)).astype(BF16)
            kt = (k * jnp.exp(mid - b)).astype(BF16)
            qb = (qs * jnp.exp(b)).astype(BF16)
            kb = (k * jnp.exp(b_end - b)).astype(BF16)
            dec = jnp.exp(b_end)
            vb = v_ref[rs, co:co + BR_W].astype(BF16)
            for h in range(HG_HEADS):
                sl = slice(h * HG_DK, (h + 1) * HG_DK)
                osl = slice(co + h * HG_DK, co + (h + 1) * HG_DK)
                att = jnp.where(mask, _dot_nt(qt[:, sl], kt[:, sl]), 0.0)
                st = st_ref[bb, h]
                o = _dot(att.astype(BF16), vb[:, sl]) + _dot_nt(qb[:, sl], st.astype(BF16))
                st_ref[bb, h] = dec[:, sl] * st + _dot_tn(vb[:, sl], kb[:, sl])
                if post:
                    o_scr[rs, osl] = o
                else:
                    o_ref[rs, osl] = o
            yield
        if post:
            nrm = nrm_ref[...]
            for h in range(HG_HEADS):
                sl = slice(h * HG_DK, (h + 1) * HG_DK)
                osl = slice(co + h * HG_DK, co + (h + 1) * HG_DK)
                o = of_ref[:, osl] + o_scr[:, osl]
                ms = jnp.mean(o * o, axis=-1, keepdims=True)
                o_ref[:, osl] = (o * lax.rsqrt(ms + EPS)) * nrm[:, sl] * _silu(z_ref[:, osl])


def _hgrn_part(uq, uv, uf, uz, of, st_in, lb, nrm, reverse):
    tp = uq.shape[0]
    tb = min(256, tp)
    nblk = tp // tb
    post = of is not None
    tix = (lambda i: nblk - 1 - i) if reverse else (lambda i: i)
    blk = pl.BlockSpec((tb, BPS * BR_W), lambda b, i: (tix(i), b))
    c2 = lambda b, i: (0, 0)
    stspec = pl.BlockSpec((BPS,) + st_in.shape[1:], lambda b, i: (b, 0, 0, 0))
    in_specs = [blk, blk, blk]
    args = [uq, uv, uf]
    scratch = []
    if post:
        in_specs += [blk, blk, pl.BlockSpec((1, BR_W), c2), pl.BlockSpec((1, BR_W), c2)]
        args += [uz, of, lb, nrm]
        scratch = [pltpu.VMEM((tb, BPS * BR_W), F32)]
    else:
        in_specs += [pl.BlockSpec((1, BR_W), c2)]
        args += [lb]
    in_specs.append(stspec)
    args.append(st_in)
    return dict(
        kern=functools.partial(_hgrn_kernel, reverse=reverse, post=post, tb=tb),
        in_specs=in_specs, args=args, st_in=[a is st_in for a in args].index(True),
        out_shape=[jax.ShapeDtypeStruct((tp, NB * BR_W), F32), jax.ShapeDtypeStruct(st_in.shape, F32)],
        out_specs=[blk, stspec],
        scratch=scratch)


def _ssd_kernel(*refs, reverse, post, tb):
    if post:
        (xs_ref, dt_ref, z_ref, of_ref, dtb_ref, a_ref, e_ref, dsk_ref, nrm_ref,
         st_in_ref, o_ref, st_ref, o_scr) = refs
    else:
        (xs_ref, dt_ref, dtb_ref, a_ref, e_ref, st_in_ref, o_ref, st_ref) = refs

    gn = M2_GROUPS * M2_STATE
    ri = lax.broadcasted_iota(jnp.int32, (CHUNK, BR_W), 0)
    cj = lax.broadcasted_iota(jnp.int32, (CHUNK, BR_W), 1) % CHUNK
    mask = (cj >= ri) if reverse else (cj <= ri)
    teye = (cj == ri).astype(F32)
    r8 = lax.broadcasted_iota(jnp.int32, (BR_W, BR_W), 0) // M2_HEADDIM
    c8 = lax.broadcasted_iota(jnp.int32, (BR_W, BR_W), 1) // M2_HEADDIM
    bdmask = r8 == c8
    hpg = M2_HEADS // M2_GROUPS
    g_row = lax.broadcasted_iota(jnp.int32, (BR_W, gn), 0) // (hpg * CHUNK)
    g_col = lax.broadcasted_iota(jnp.int32, (BR_W, gn), 1) // M2_STATE
    bmask = g_row == g_col
    s_row = lax.broadcasted_iota(jnp.int32, (gn, BR_W), 0) // M2_STATE
    s_col = lax.broadcasted_iota(jnp.int32, (gn, BR_W), 1) // (hpg * M2_HEADDIM)
    smask = s_row == s_col
    zero = jnp.zeros((), BF16)

    nchunk = tb // CHUNK
    order = range(nchunk - 1, -1, -1) if reverse else range(nchunk)
    for bb in range(BPS):
        xo = bb * M2_XBC
        osl = slice(bb * BR_W, (bb + 1) * BR_W)
        dt = _softplus(dt_ref[:, bb * DT_PAD:(bb + 1) * DT_PAD] + dtb_ref[...])
        cum_n = _chunk_cumsum(dt * a_ref[...], reverse)
        ex = _dot(jnp.concatenate(_split3(cum_n) + _split3(dt), axis=0), e_ref[...])
        cum_all = _sum3(ex[0:3 * tb], tb, 0)
        dt_all = _sum3(ex[3 * tb:6 * tb], tb, 0)
        for c in order:
            rs = slice(c * CHUNK, (c + 1) * CHUNK)
            xh = xs_ref[rs, xo:xo + BR_W]
            bm = xs_ref[rs, xo + BR_W:xo + BR_W + gn]
            cm = xs_ref[rs, xo + BR_W + gn:xo + BR_W + 2 * gn]
            cum = cum_all[rs, :]
            dt_e = dt_all[rs, :]
            cum_end = cum[0:1, :] if reverse else cum[CHUNK - 1:CHUNK, :]
            cum_row = jnp.sum(cum * teye, axis=0, keepdims=True)
            dt_row = jnp.sum(dt_e * teye, axis=0, keepdims=True)
            decay = jnp.exp(jnp.where(mask, cum - cum_row, -jnp.inf))
            bm8 = jnp.where(bmask, jnp.concatenate([bm] * M2_HEADS, axis=0), zero)
            scores = _dot_nt(cm, bm8)
            w = (scores * decay * dt_row).astype(BF16)
            xbd = jnp.where(bdmask, jnp.concatenate([xh] * M2_HEADS, axis=0), zero)
            st = st_ref[bb]
            y = _dot(w, xbd) + _dot(cm, st.astype(BF16)) * jnp.exp(cum)
            wx = (xh.astype(F32) * (jnp.exp(cum_end - cum) * dt_e)).astype(BF16)
            st_ref[bb] = jnp.exp(cum_end) * st + jnp.where(smask, _dot_tn(bm, wx), 0.0)
            if post:
                o_scr[rs, osl] = y
            else:
                o_ref[rs, osl] = y
            yield
        if post:
            y = of_ref[:, osl] + o_scr[:, osl] + dsk_ref[...] * xs_ref[:, xo:xo + BR_W].astype(F32)
            yz = y * _silu(z_ref[:, osl])
            ms = jnp.mean(yz * yz, axis=-1, keepdims=True)
            o_ref[:, osl] = (yz * lax.rsqrt(ms + EPS)) * nrm_ref[...]


def _ssd_part(uxs, udt, uz, of, st_in, dtb, a, e, dsk, nrm, reverse):
    tp = uxs.shape[0]
    tb = min(256, tp)
    nblk = tp // tb
    post = of is not None
    gn = M2_GROUPS * M2_STATE
    tix = (lambda i: nblk - 1 - i) if reverse else (lambda i: i)
    tile = lambda w: pl.BlockSpec((tb, BPS * w), lambda b, i: (tix(i), b))
    c2 = lambda b, i: (0, 0)
    stspec = pl.BlockSpec((BPS, gn, BR_W), lambda b, i: (b, 0, 0))
    in_specs = [tile(M2_XBC), tile(DT_PAD)]
    args = [uxs, udt]
    scratch = []
    if post:
        in_specs += [tile(BR_W), tile(BR_W)]
        args += [uz, of]
        scratch = [pltpu.VMEM((tb, BPS * BR_W), F32)]
    in_specs += [pl.BlockSpec((1, DT_PAD), c2), pl.BlockSpec((1, DT_PAD), c2),
                 pl.BlockSpec((DT_PAD, BR_W), c2)]
    args += [dtb, a, e]
    if post:
        in_specs += [pl.BlockSpec((1, BR_W), c2), pl.BlockSpec((1, BR_W), c2)]
        args += [dsk, nrm]
    in_specs.append(stspec)
    args.append(st_in)
    return dict(
        kern=functools.partial(_ssd_kernel, reverse=reverse, post=post, tb=tb),
        in_specs=in_specs, args=args, st_in=[a is st_in for a in args].index(True),
        out_shape=[jax.ShapeDtypeStruct((tp, NB * BR_W), F32), jax.ShapeDtypeStruct(st_in.shape, F32)],
        out_specs=[tile(BR_W), stspec],
        scratch=scratch)


def _s5_params(a_re, a_im, log_step, b_re, b_im, c_re, c_im):
    nl, g, n, p = b_re.shape
    gpc = g // S5_NCH
    m = S5_BLK
    ar, ai = a_re.astype(F32), a_im.astype(F32)
    step = jnp.exp(log_step.astype(F32))[..., None]
    mag = jnp.exp(ar * step)
    ab_re, ab_im = mag * jnp.cos(ai * step), mag * jnp.sin(ai * step)
    den = ar * ar + ai * ai
    co_re = ((ab_re - 1.0) * ar + ab_im * ai) / den
    co_im = (ab_im * ar - (ab_re - 1.0) * ai) / den
    br, bi = b_re.astype(F32), b_im.astype(F32)
    cr, ci = c_re.astype(F32), c_im.astype(F32)
    pr, pi = [jnp.ones_like(ab_re)], [jnp.zeros_like(ab_im)]
    for _ in range(m):
        pr, pi = pr + [pr[-1] * ab_re - pi[-1] * ab_im], pi + [pr[-1] * ab_im + pi[-1] * ab_re]
    pw_re, pw_im = jnp.stack(pr, axis=2), jnp.stack(pi, axis=2)
    steps = np.arange(m)
    e_idx = np.stack([m - 1 - steps, steps])
    g_idx = np.stack([steps + 1, m - steps])
    k_idx = np.abs(steps[:, None] - steps[None, :])
    allow = np.stack([steps[:, None] <= steps[None, :], steps[:, None] >= steps[None, :]]).astype(np.float32)

    def per_dir(pw, idx):
        return jnp.stack([pw[:, d][:, idx[d]] for d in range(2)], axis=1)

    eye = jnp.eye(gpc, dtype=F32)
    ch = lambda t, ax: t.reshape(t.shape[:ax] + (S5_NCH, gpc) + t.shape[ax + 1:])
    er, ei = ch(per_dir(pw_re, e_idx), 3), ch(per_dir(pw_im, e_idx), 3)
    brc, bic = ch(br, 1)[:, None, None], ch(bi, 1)[:, None, None]
    wd_blk = lambda t: jnp.einsum('ldicgnp,gh->ldcigphn', t, eye).reshape(
        nl, 2, S5_NCH, m * gpc * p, gpc * n)
    wd = jnp.concatenate([wd_blk(er[..., None] * brc - ei[..., None] * bic),
                          wd_blk(er[..., None] * bic + ei[..., None] * brc)], axis=-1).astype(BF16)
    cc_re = cr[:, None] * co_re[:, :, :, None, :] - ci[:, None] * co_im[:, :, :, None, :]
    cc_im = cr[:, None] * co_im[:, :, :, None, :] + ci[:, None] * co_re[:, :, :, None, :]
    gr, gi = per_dir(pw_re, g_idx)[:, :, :, :, None, :], per_dir(pw_im, g_idx)[:, :, :, :, None, :]
    wy_blk = lambda t: jnp.einsum('ldjcgpn,gh->ldcgnjhp', ch(t, 3), eye).reshape(
        nl, 2, S5_NCH, gpc * n, m * gpc * p)
    wy = jnp.concatenate([wy_blk(cc_re[:, :, None] * gr - cc_im[:, :, None] * gi),
                          wy_blk(-(cc_re[:, :, None] * gi + cc_im[:, :, None] * gr))], axis=-2).astype(BF16)
    kr, ki = pw_re[:, :, 0:m, :, None, :], pw_im[:, :, 0:m, :, None, :]
    car = cc_re[:, :, None] * kr - cc_im[:, :, None] * ki
    cai = cc_re[:, :, None] * ki + cc_im[:, :, None] * kr
    thru = jnp.einsum('ldkgon,lgni->ldkgio', car, br) - jnp.einsum('ldkgon,lgni->ldkgio', cai, bi)
    tij = thru[:, :, k_idx] * allow[None, :, :, :, None, None, None]
    wf = jnp.einsum('ldijcgab,gh->ldcigajhb', ch(tij, 4), eye).reshape(
        nl, 2, S5_NCH, m * gpc * p, m * gpc * p).astype(BF16)
    ab = jnp.concatenate([pw_re[:, :, m].reshape(nl, 2, S5_NCH, gpc * n),
                          pw_im[:, :, m].reshape(nl, 2, S5_NCH, gpc * n)], axis=-1)
    ab = jnp.broadcast_to(ab[:, :, :, None, :], (nl, 2, S5_NCH, NB, 2 * gpc * n))
    return wd, ab, wy, wf


def _lane_vec(v, d):
    out = jnp.zeros((1, DT_PAD), F32)
    return lax.dynamic_update_slice(out, v.astype(F32)[None, :], (0, M2_HEADS * d))


def _head_expand(d):
    r = jnp.arange(DT_PAD)[:, None]
    c = jnp.arange(BR_W)[None, :] // M2_HEADDIM
    return (r == c + M2_HEADS * d).astype(BF16)


def kernel(x, c, ctx, c_ctx, norm_w, w_mod, b_mod, w_in, hg_lb_logits, hg_norm, s5_a_re, s5_a_im, s5_log_step, s5_b_re, s5_b_im, s5_c_re, s5_c_im, s5_d, s5_w_glu, s5_b_glu, lru_conv_w, lru_conv_b, lru_gate_w, lru_gate_b, lru_lam, m2_conv_w, m2_conv_b, m2_dt_bias, m2_a_log, m2_d, m2_norm, w_branch, w_gate, b_gate, w_out, final_norm):
    bsz, seq, dm = x.shape
    n_ctx = ctx.shape[1]
    depth = norm_w.shape[0]
    assert bsz == NB and seq % 256 == 0 and n_ctx % CHUNK == 0

    xl, xc = x, ctx
    c_all = jnp.concatenate([c, jnp.broadcast_to(c_ctx[None, :], (NB, dm))], axis=0)
    lb_all = jnp.cumsum(jax.nn.softmax(hg_lb_logits.astype(F32), axis=0), axis=0)
    perm = _tile_perm()
    perm_t = perm.T

    o_s5 = 5 * BR_W
    o_lru = o_s5 + 2 * BR_W
    o_lz = o_lru + BR_W
    o_m2 = o_lru + 2 * BR_W
    o_dt = o_m2 + M2_XBC
    o_mz = o_dt + 2 * M2_HEADS
    gn = M2_GROUPS * M2_STATE

    s5_w = _s5_params(s5_a_re, s5_a_im, s5_log_step, s5_b_re, s5_b_im, s5_c_re, s5_c_im)
    nbh = lru_gate_w.shape[3] // LRU_HALVES
    bw = lru_gate_w.shape[-1]
    gw = (0.5 * lru_gate_w).reshape(depth, 2, 2, LRU_HALVES, nbh, bw, bw)
    lru_wg = jnp.einsum('ldgjnab,nm->ldjnagmb', gw, jnp.eye(nbh, dtype=F32)).reshape(
        depth, 2, LRU_HALVES, nbh * bw, 2 * nbh * bw).astype(BF16)
    lru_bg = jnp.transpose((0.5 * lru_gate_b).reshape(depth, 2, 2, LRU_HALVES, nbh * bw),
                           (0, 1, 3, 2, 4)).reshape(depth, 2, LRU_HALVES, 1, 2 * nbh * bw)

    for l in range(depth):
        mod = _mod_call(c_all, w_mod[l].astype(BF16), b_mod[l][None, :])
        nw = norm_w[l][None, :]
        wl = w_in[l]
        w_s5 = wl[:, o_s5:o_lru].astype(BF16)
        w_lx = wl[:, o_lru:o_lz].astype(BF16)
        w_lz = wl[:, o_lz:o_m2].astype(BF16)
        w_mx = wl[:, o_m2:o_dt].astype(BF16)
        w_bm = jnp.concatenate(
            [wl[:, 0:o_s5], jnp.pad(wl[:, o_dt:o_mz], ((0, 0), (0, DT_PAD - 2 * M2_HEADS))), wl[:, o_mz:]],
            axis=1).astype(BF16)
        wg = w_gate[l].astype(BF16)
        bg = b_gate[l][:, None, :]
        wb = w_branch[l].astype(BF16)
        wo = w_out[l].astype(BF16)

        parts = []
        for name, x3, row0, period in (("ctx", xc, NB, n_ctx), ("lat", xl, 0, CHUNK)):
            sh = mod[row0:row0 + NB, 0:dm]
            sc = mod[row0:row0 + NB, dm:2 * dm]
            gt = mod[row0:row0 + NB, 2 * dm:3 * dm]
            ub, ucx, ucz, uxs, uq, uv, uff, ufb, uaz, udt, udz = _proj_call(
                x3, nw, sc, sh, perm, perm_t, w_s5, w_lx, w_lz, lru_conv_w[l], lru_conv_b[l][None, :],
                w_mx, w_bm, (BR_W,) * 5 + (DT_PAD, BR_W), m2_conv_w[l], m2_conv_b[l][None, :],
                period // CHUNK, "proj_" + name)
            parts.append(dict(name=name, x=x3, sc=sc, sh=sh, gt=gt, uq=uq, uv=uv, uf=(uff, ufb), uaz=uaz,
                              ub=ub, ucx=ucx, ucz=ucz, uxs=uxs, udt=udt, udz=udz))

        ys = [dict(), dict()]
        for d, reverse in ((0, False), (1, True)):
            tag = "bwd" if reverse else "fwd"
            st_a = jnp.zeros((NB, HG_HEADS, HG_DK, HG_DK), F32)
            st_b = jnp.zeros((NB, 2 * (BR_W // S5_GROUP) * S5_STATE), F32)
            st_c = jnp.zeros((NB, BR_W), F32)
            st_d = jnp.zeros((NB, gn, BR_W), F32)
            for pi, p in enumerate(parts):
                nm = tag + "_" + p["name"]
                of = ys[pi] if reverse else dict(a=None, b=None, c=None, d=None)
                r = p["ub"].shape[0]
                tp = r // NB
                (ob, st_b), (oc, st_c) = _fused_call(
                    [_s5_part(p["ub"], of["b"], st_b, *s5_w, (l, d), s5_d[l][None, :],
                              s5_w_glu[l].astype(BF16), s5_b_glu[l][None, :], reverse),
                     _lru_part(p["ucx"], p["ucz"], of["c"], st_c, lru_wg, lru_bg, (l, d),
                               lru_lam[l, d][None, :], reverse)],
                    (r // TILE,), "tm_" + nm)
                (oa, st_a), (od, st_d) = _fused_call(
                    [_hgrn_part(p["uq"], p["uv"], p["uf"][d], p["uaz"], of["a"], st_a,
                                lb_all[l, d][None, :], hg_norm[l][None, :], reverse),
                     _ssd_part(p["uxs"], p["udt"], p["udz"], of["d"], st_d,
                               _lane_vec(m2_dt_bias[l, d], d),
                               _lane_vec(-jnp.exp(m2_a_log[l, d].astype(F32)), d), _head_expand(d),
                               jnp.repeat(m2_d[l].astype(F32), M2_HEADDIM)[None, :],
                               m2_norm[l][None, :], reverse)],
                    (NB // BPS, tp // min(256, tp)), "bm_" + nm)
                ys[pi] = dict(a=oa, b=ob, c=oc, d=od)

        last = l == depth - 1
        for pi, p in enumerate(parts):
            if last and p["name"] == "ctx":
                continue
            y = ys[pi]
            out = _merge_call(p["x"], (y["a"], y["b"], y["c"], y["d"]), nw, p["sc"], p["sh"], p["gt"],
                              perm, wg, bg, wb, wo, final_norm[None, :], last, "merge_" + p["name"])
            if p["name"] == "ctx":
                xc = out
            else:
                xl = out
    return xl
```

```python
import functools
import math

import jax
import jax.numpy as jnp
import numpy as np
from jax import lax
from jax.experimental import pallas as pl
from jax.experimental.pallas import tpu as pltpu

F32 = jnp.float32
BF16 = jnp.bfloat16

NB = 8
EPS = 1e-6
CHUNK = 64
TILE = CHUNK * NB
HALO = 2 * NB
CONV_W = 4
BR_W = 512
HG_HEADS = 4
HG_DK = BR_W // HG_HEADS
S5_GROUP = 16
S5_STATE = 64
S5_NCH = 4
S5_BLK = 4
LRU_C = 8.0
LRU_HALVES = 2
M2_HEADDIM = 64
M2_HEADS = BR_W // M2_HEADDIM
M2_GROUPS = 2
M2_STATE = 64
M2_XBC = BR_W + 2 * M2_GROUPS * M2_STATE
DT_PAD = 128
BPS = 4
VMEM_LIMIT = 56 * 1024 * 1024


def _cp(n_axes):
    return pltpu.CompilerParams(dimension_semantics=("arbitrary",) * n_axes,
                                vmem_limit_bytes=VMEM_LIMIT)


def _const_spec(shape):
    nd = len(shape)
    return pl.BlockSpec(shape, lambda *_: (0,) * nd, pipeline_mode=pl.Buffered(1))


def _dot(a, b):
    return jnp.dot(a, b, preferred_element_type=F32)


def _dot_nt(a, b):
    return lax.dot_general(a, b, (((1,), (1,)), ((), ())), preferred_element_type=F32)


def _dot_tn(a, b):
    return lax.dot_general(a, b, (((0,), (0,)), ((), ())), preferred_element_type=F32)


def _split3(x):
    hi = x.astype(BF16)
    r1 = x - hi.astype(F32)
    mid = r1.astype(BF16)
    lo = (r1 - mid.astype(F32)).astype(BF16)
    return hi, mid, lo


def _sum3(y, n, axis):
    if axis == 1:
        return (y[:, 0:n] + y[:, n:2 * n]) + y[:, 2 * n:3 * n]
    return (y[0:n] + y[n:2 * n]) + y[2 * n:3 * n]


def _chunk_cumsum(x, reverse):
    n = x.shape[0]
    i = lax.broadcasted_iota(jnp.int32, (n, n), 0)
    j = lax.broadcasted_iota(jnp.int32, (n, n), 1)
    tri = ((i // CHUNK == j // CHUNK) & ((j >= i) if reverse else (j <= i))).astype(BF16)
    y = _dot(tri, jnp.concatenate(_split3(x), axis=1))
    return _sum3(y, x.shape[1], 1)


def _sigmoid(x):
    return 0.5 * jnp.tanh(0.5 * x) + 0.5


def _silu(x):
    return x * _sigmoid(x)


def _softplus(x):
    return jnp.maximum(x, 0.0) + jnp.log1p(jnp.exp(-jnp.abs(x)))


def _adaln(x3, nw, sc, sh):
    nb, n, d = x3.shape
    ms = jnp.mean(x3 * x3, axis=-1, keepdims=True)
    xn = (x3 * lax.rsqrt(ms + EPS)) * nw
    h = xn * (1.0 + sc)[:, None, :] + sh[:, None, :]
    return h.reshape(nb * n, d)


def _tile_perm():
    i = jnp.arange(TILE)
    src = (i % CHUNK) * NB + i // CHUNK
    return (src[:, None] == jnp.arange(TILE)[None, :]).astype(BF16)


def _halo_select(first_step):
    i = jnp.arange(HALO)
    src = (i % NB) * NB + first_step + i // NB
    return (src[:, None] == jnp.arange(NB * NB)[None, :]).astype(BF16)


def _mod_kernel(c_ref, w_ref, b_ref, o_ref):
    s = _silu(c_ref[...]).astype(BF16)
    o_ref[...] = _dot(s, w_ref[...]) + b_ref[...]


def _mod_call(c_all, w, b):
    n, d = c_all.shape
    m = w.shape[1]
    return pl.pallas_call(
        _mod_kernel,
        out_shape=jax.ShapeDtypeStruct((n, m), F32),
        grid=(1,),
        in_specs=[pl.BlockSpec((n, d), lambda i: (0, 0)),
                  pl.BlockSpec((d, m), lambda i: (0, 0)),
                  pl.BlockSpec((1, m), lambda i: (0, 0))],
        out_specs=pl.BlockSpec((n, m), lambda i: (0, 0)),
        compiler_params=_cp(1),
        name="mod",
    )(c_all, w, b)


def _conv_tm(hb, halo, w_ref, cw_ref, cb_ref, ptiles):
    u = _dot(hb, w_ref[...])
    width = u.shape[1]
    if ptiles == 1:
        up = jnp.zeros((NB, width), F32)
        un = jnp.zeros((HALO, width), F32)
    else:
        i = pl.program_id(0)
        vp = (i % ptiles != 0).astype(F32)
        vn = (i % ptiles != ptiles - 1).astype(F32)
        up = _dot(halo[0], w_ref[...])[NB:HALO, :] * vp
        un = _dot(halo[1], w_ref[...]) * vn
    ue = jnp.concatenate([up, u, un], axis=0)
    xc = cb_ref[...] + cw_ref[0:1, :] * ue[0:TILE, :]
    for k in range(1, CONV_W):
        xc = xc + cw_ref[k:k + 1, :] * ue[k * NB:k * NB + TILE, :]
    return xc


def _store_bm(o_ref, u, wd, c0):
    cw = u.shape[1]
    for b in range(NB):
        o_ref[:, b * wd + c0:b * wd + c0 + cw] = u[b * CHUNK:(b + 1) * CHUNK, :].astype(o_ref.dtype)


def _proj_kernel(*refs, ptiles, widths):
    if ptiles == 1:
        x_ref, nw_ref, sc_ref, sh_ref, p_ref, pt_ref = refs[:6]
        rest = refs[6:]
        halo = None
    else:
        x_ref, xp_ref, xn_ref, nw_ref, sc_ref, sh_ref, p_ref, pt_ref, sp_ref, sn_ref = refs[:10]
        rest = refs[10:]
        halo = tuple(
            _dot(s_ref[...], _adaln(r[...], nw_ref[...], sc_ref[...], sh_ref[...]).astype(BF16)).astype(BF16)
            for s_ref, r in ((sp_ref, xp_ref), (sn_ref, xn_ref)))
    (w_s5_ref, w_lx_ref, w_lz_ref, cwl_ref, cbl_ref, w_x_ref, w_ref, cwm_ref, cbm_ref,
     ub_ref, ucx_ref, ucz_ref, uxs_ref) = rest[:13]
    o_refs = rest[13:]
    hb_bm = _adaln(x_ref[...], nw_ref[...], sc_ref[...], sh_ref[...]).astype(BF16)
    hb_tm = _dot(pt_ref[...], hb_bm).astype(BF16)
    ub_ref[...] = _dot(hb_tm, w_s5_ref[...])
    ucx_ref[...] = _conv_tm(hb_tm, halo, w_lx_ref, cwl_ref, cbl_ref, ptiles)
    ucz_ref[...] = _dot(hb_tm, w_lz_ref[...])
    xs = _silu(_conv_tm(hb_tm, halo, w_x_ref, cwm_ref, cbm_ref, ptiles)).astype(BF16)
    _store_bm(uxs_ref, _dot(p_ref[...], xs), M2_XBC, 0)
    off = 0
    for o_ref, wd in zip(o_refs, widths):
        for c0 in range(0, wd, BR_W):
            cw = min(BR_W, wd - c0)
            _store_bm(o_ref, _dot(hb_bm, w_ref[:, off + c0:off + c0 + cw]), wd, c0)
        off += wd


def _proj_call(x3, nw, sc, sh, perm, perm_t, w_s5, w_lx, w_lz, cwl, cbl, w_x, w, widths, cwm, cbm,
               ptiles, name):
    nb, t, d = x3.shape
    r = nb * t
    blk = lambda i: (0, i, 0)
    row = lambda i: (i, 0)
    in_specs = [pl.BlockSpec((NB, CHUNK, d), blk)]
    args = [x3]
    if ptiles > 1:
        per = CHUNK // NB
        last = t // NB - 1
        in_specs += [pl.BlockSpec((NB, NB, d), lambda i: (0, jnp.maximum(i * per - 1, 0), 0)),
                     pl.BlockSpec((NB, NB, d), lambda i: (0, jnp.minimum((i + 1) * per, last), 0))]
        args += [x3, x3]
    consts = [nw, sc, sh, perm, perm_t]
    if ptiles > 1:
        consts += [_halo_select(NB - 2), _halo_select(0)]
    consts += [w_s5, w_lx, w_lz, cwl, cbl, w_x, w, cwm, cbm]
    for a in consts:
        in_specs.append(_const_spec(a.shape))
        args.append(a)
    tm_w = (2 * BR_W, BR_W, BR_W)
    bm = [(M2_XBC, BF16)] + [(wd, F32) for wd in widths]
    return pl.pallas_call(
        functools.partial(_proj_kernel, ptiles=ptiles, widths=widths),
        out_shape=[jax.ShapeDtypeStruct((r, wd), F32) for wd in tm_w]
                  + [jax.ShapeDtypeStruct((t, NB * wd), dt) for wd, dt in bm],
        grid=(t // CHUNK,),
        in_specs=in_specs,
        out_specs=[pl.BlockSpec((TILE, wd), row) for wd in tm_w]
                  + [pl.BlockSpec((CHUNK, NB * wd), row) for wd, _ in bm],
        compiler_params=_cp(1),
        name=name,
    )(*args)


def _merge_kernel(x_ref, ya_ref, yb_ref, yc_ref, yd_ref, nw_ref, sc_ref, sh_ref, gt_ref, p_ref,
                  wg_ref, bg_ref, wb_ref, wo_ref, fn_ref, o_ref, *, final):
    x3 = x_ref[...]
    hb = _adaln(x3, nw_ref[...], sc_ref[...], sh_ref[...]).astype(BF16)

    def from_bm(y_ref):
        return jnp.concatenate([y_ref[:, b * BR_W:(b + 1) * BR_W] for b in range(NB)],
                               axis=0).astype(BF16)

    def from_tm(y_ref):
        return _dot(p_ref[...], y_ref[...].astype(BF16)).astype(BF16)

    ys = (from_bm(ya_ref), from_tm(yb_ref), from_tm(yc_ref), from_bm(yd_ref))
    m = None
    for k, y in enumerate(ys):
        g = _sigmoid(_dot(hb, wg_ref[k]) + bg_ref[k])
        p = _dot(y, wb_ref[k])
        m = g * p if m is None else m + g * p
    upd = _dot(m.astype(BF16), wo_ref[...])
    out = x3 + upd.reshape(x3.shape) * gt_ref[...][:, None, :]
    if final:
        ms = jnp.mean(out * out, axis=-1, keepdims=True)
        out = (out * lax.rsqrt(ms + EPS)) * fn_ref[...]
    o_ref[...] = out


def _merge_call(x3, ys, nw, sc, sh, gt, perm, wg, bg, wb, wo, fn, final, name):
    nb, t, d = x3.shape
    blk = lambda i: (0, i, 0)
    row = lambda i: (i, 0)
    tm_spec = pl.BlockSpec((TILE, BR_W), row)
    bm_spec = pl.BlockSpec((CHUNK, NB * BR_W), row)
    return pl.pallas_call(
        functools.partial(_merge_kernel, final=final),
        out_shape=jax.ShapeDtypeStruct(x3.shape, F32),
        grid=(t // CHUNK,),
        in_specs=[pl.BlockSpec((NB, CHUNK, d), blk), bm_spec, tm_spec, tm_spec, bm_spec,
                  _const_spec((1, d)), _const_spec((NB, d)), _const_spec((NB, d)),
                  _const_spec((NB, d)), _const_spec((TILE, TILE)),
                  _const_spec(wg.shape), _const_spec(bg.shape), _const_spec(wb.shape),
                  _const_spec(wo.shape), _const_spec((1, d))],
        out_specs=pl.BlockSpec((NB, CHUNK, d), blk),
        compiler_params=_cp(1),
        name=name,
    )(x3, *ys, nw, sc, sh, gt, perm, wg, bg, wb, wo, fn)


def _fused_call(parts, grid, name):
    n_in = [len(p["args"]) for p in parts]
    n_out = [len(p["out_shape"]) for p in parts]
    n_scr = [len(p["scratch"]) for p in parts]

    def kern(*refs):
        ins = refs[:sum(n_in)]
        outs = refs[sum(n_in):sum(n_in) + sum(n_out)]
        scr = refs[sum(n_in) + sum(n_out):]
        split = []
        i = o = s = 0
        for a, b, c in zip(n_in, n_out, n_scr):
            split.append((ins[i:i + a], outs[o:o + b], scr[s:s + c]))
            i, o, s = i + a, o + b, s + c

        @pl.when(pl.program_id(len(grid) - 1) == 0)
        def _():
            for p, (pin, pout, _) in zip(parts, split):
                pout[1][...] = pin[p["st_in"]][...]

        active = [p["kern"](*pin, *pout, *pscr) for p, (pin, pout, pscr) in zip(parts, split)]
        while active:
            for g in list(active):
                if next(g, StopIteration) is StopIteration:
                    active.remove(g)

    res = pl.pallas_call(
        kern,
        out_shape=[x for p in parts for x in p["out_shape"]],
        grid=grid,
        in_specs=[x for p in parts for x in p["in_specs"]],
        out_specs=[x for p in parts for x in p["out_specs"]],
        scratch_shapes=[x for p in parts for x in p["scratch"]],
        compiler_params=_cp(len(grid)),
        name=name,
    )(*[x for p in parts for x in p["args"]])
    out, o = [], 0
    for b in n_out:
        out.append(res[o:o + b])
        o += b
    return out


def _lru_kernel(*refs, reverse, post):
    if post:
        (x_ref, z_ref, of_ref, st_in_ref, wg_ref, bg_ref, lam_ref, o_ref, st_ref, a_scr, b_scr) = refs
    else:
        (x_ref, st_in_ref, wg_ref, bg_ref, lam_ref, o_ref, st_ref, a_scr, b_scr) = refs

    c1 = (-0.5 * LRU_C) * _softplus(-lam_ref[...])
    nq = 4
    qr = TILE // nq
    hw = BR_W // LRU_HALVES
    for q in (range(nq - 1, -1, -1) if reverse else range(nq)):
        rs = slice(q * qr, (q + 1) * qr)
        for j in range(LRU_HALVES):
            cs = slice(j * hw, (j + 1) * hw)
            xc = x_ref[rs, cs]
            th = jnp.tanh(_dot(xc.astype(BF16), wg_ref[j]) + bg_ref[j])
            log_a = c1[:, cs] * th[:, 0:hw] + c1[:, cs]
            a = jnp.exp(log_a)
            a_scr[rs, cs] = a
            v = -jnp.tanh(log_a) * (a * a + 1.0)
            root = jnp.where(v > 0.0, v * lax.rsqrt(v), 0.0)
            b_scr[rs, cs] = root * ((0.5 * th[:, hw:2 * hw] + 0.5) * xc)
        yield
    h = st_ref[...]
    for i in range(CHUNK):
        r0 = ((CHUNK - 1 - i) if reverse else i) * NB
        h = a_scr[r0:r0 + NB, :] * h + b_scr[r0:r0 + NB, :]
        b_scr[r0:r0 + NB, :] = h
        yield
    st_ref[...] = h
    if post:
        o_ref[...] = (of_ref[...] + b_scr[...]) * _silu(z_ref[...])
    else:
        o_ref[...] = b_scr[...]


def _pick_spec(a, lead):
    k = len(lead)
    return pl.BlockSpec((None,) * k + a.shape[k:], lambda *_: tuple(lead) + (0,) * (a.ndim - k))


def _lru_part(xc, z, of, st_in, wg, bg, ld, lam, reverse):
    r = xc.shape[0]
    nblk = r // TILE
    post = of is not None
    tmap = (lambda i: (nblk - 1 - i, 0)) if reverse else (lambda i: (i, 0))
    c2 = lambda i: (0, 0)
    tile = pl.BlockSpec((TILE, BR_W), tmap)
    in_specs = [tile]
    args = [xc]
    if post:
        in_specs += [tile, tile]
        args += [z, of]
    in_specs += [pl.BlockSpec((NB, BR_W), c2), _pick_spec(wg, ld), _pick_spec(bg, ld),
                 pl.BlockSpec((1, BR_W), c2)]
    args += [st_in, wg, bg, lam]
    return dict(
        kern=functools.partial(_lru_kernel, reverse=reverse, post=post),
        in_specs=in_specs, args=args, st_in=[a is st_in for a in args].index(True),
        out_shape=[jax.ShapeDtypeStruct((r, BR_W), F32), jax.ShapeDtypeStruct((NB, BR_W), F32)],
        out_specs=[tile, pl.BlockSpec((NB, BR_W), c2)],
        scratch=[pltpu.VMEM((TILE, BR_W), F32), pltpu.VMEM((TILE, BR_W), F32)])


def _gelu_tanh(x):
    return 0.5 * x * (1.0 + jnp.tanh(math.sqrt(2.0 / math.pi) * (x + 0.044715 * (x * x * x))))


def _s5_kernel(*refs, reverse, post):
    if post:
        (u_ref, of_ref, st_in_ref, wd_ref, ab_ref, wy_ref, wf_ref, dsk_ref, wglu_ref, bglu_ref,
         o_ref, st_ref, s_scr, y_scr) = refs
    else:
        (u_ref, st_in_ref, wd_ref, ab_ref, wy_ref, wf_ref, o_ref, st_ref, s_scr) = refs
        y_scr = o_ref

    half = (BR_W // S5_GROUP) * S5_STATE // S5_NCH
    cin = BR_W // S5_NCH
    nk = CHUNK // S5_BLK
    u4s = []
    for c in range(S5_NCH):
        ug = u_ref[:, c * cin:(c + 1) * cin].reshape(nk, S5_BLK, NB, cin)
        u4 = jnp.concatenate([ug[:, i].reshape(nk * NB, cin) for i in range(S5_BLK)], axis=1).astype(BF16)
        u4s.append(u4)
        s_scr[c] = _dot(u4, wd_ref[c])
        yield
    for c in range(S5_NCH):
        a_re = ab_ref[c, :, 0:half]
        a_im = ab_ref[c, :, half:2 * half]
        c0 = c * 2 * half
        s_re = st_ref[:, c0:c0 + half]
        s_im = st_ref[:, c0 + half:c0 + 2 * half]
        for kk in range(nk):
            r0 = ((nk - 1 - kk) if reverse else kk) * NB
            n_re = a_re * s_re - a_im * s_im + s_scr[c, r0:r0 + NB, 0:half]
            n_im = a_re * s_im + a_im * s_re + s_scr[c, r0:r0 + NB, half:2 * half]
            s_scr[c, r0:r0 + NB, 0:half] = s_re
            s_scr[c, r0:r0 + NB, half:2 * half] = s_im
            s_re, s_im = n_re, n_im
            yield
        st_ref[:, c0:c0 + half] = s_re
        st_ref[:, c0 + half:c0 + 2 * half] = s_im
        y4 = _dot(s_scr[c].astype(BF16), wy_ref[c]) + _dot(u4s[c], wf_ref[c])
        y = jnp.stack([y4[:, i * cin:(i + 1) * cin].reshape(nk, NB, cin) for i in range(S5_BLK)], axis=1)
        y_scr[:, c * cin:(c + 1) * cin] = y.reshape(TILE, cin)
        yield
    if post:
        u = u_ref[:, 0:BR_W]
        z = u_ref[:, BR_W:2 * BR_W]
        y = of_ref[...] + y_scr[...] + dsk_ref[...] * u
        g = _gelu_tanh(y)
        gl = _dot(g.astype(BF16), wglu_ref[...]) + bglu_ref[...]
        o_ref[...] = g * _sigmoid(gl) * _silu(z)


def _s5_part(u, of, st_in, wd, ab, wy, wf, ld, dsk, wglu, bglu, reverse):
    r = u.shape[0]
    nblk = r // TILE
    post = of is not None
    nst = st_in.shape[1]
    tmap = (lambda i: (nblk - 1 - i, 0)) if reverse else (lambda i: (i, 0))
    c2 = lambda i: (0, 0)
    in_specs = [pl.BlockSpec((TILE, 2 * BR_W), tmap)]
    args = [u]
    if post:
        in_specs.append(pl.BlockSpec((TILE, BR_W), tmap))
        args.append(of)
    in_specs += [pl.BlockSpec((NB, nst), c2)] + [_pick_spec(a, ld) for a in (wd, ab, wy, wf)]
    args += [st_in, wd, ab, wy, wf]
    scratch = [pltpu.VMEM((S5_NCH, TILE // S5_BLK, nst // S5_NCH), F32)]
    if post:
        in_specs += [pl.BlockSpec((1, BR_W), c2), pl.BlockSpec((BR_W, BR_W), c2),
                     pl.BlockSpec((1, BR_W), c2)]
        args += [dsk, wglu, bglu]
        scratch.append(pltpu.VMEM((TILE, BR_W), F32))
    return dict(
        kern=functools.partial(_s5_kernel, reverse=reverse, post=post),
        in_specs=in_specs, args=args, st_in=[a is st_in for a in args].index(True),
        out_shape=[jax.ShapeDtypeStruct((r, BR_W), F32), jax.ShapeDtypeStruct((NB, nst), F32)],
        out_specs=[pl.BlockSpec((TILE, BR_W), tmap), pl.BlockSpec((NB, nst), c2)],
        scratch=scratch)


def _tri_mask(n, reverse):
    i = lax.broadcasted_iota(jnp.int32, (n, n), 0)
    j = lax.broadcasted_iota(jnp.int32, (n, n), 1)
    return (j >= i) if reverse else (j <= i)


def _hgrn_kernel(*refs, reverse, post, tb):
    if post:
        (q_ref, v_ref, f_ref, z_ref, of_ref, lb_ref, nrm_ref, st_in_ref,
         o_ref, st_ref, o_scr) = refs
    else:
        (q_ref, v_ref, f_ref, lb_ref, st_in_ref, o_ref, st_ref) = refs

    mask = _tri_mask(CHUNK, reverse)
    lb = lb_ref[...]
    nchunk = tb // CHUNK
    order = range(nchunk - 1, -1, -1) if reverse else range(nchunk)
    for bb in range(BPS):
        co = bb * BR_W
        f_all = lb + (1.0 - lb) * _sigmoid(f_ref[:, co:co + BR_W])
        b_all = _chunk_cumsum(jnp.log(f_all), reverse)
        for c in order:
            rs = slice(c * CHUNK, (c + 1) * CHUNK)
            k = 1.0 - f_all[rs, :]
            qs = _silu(q_ref[rs, co:co + BR_W])
            b = b_all[rs, :]
            b_end = b[0:1, :] if reverse else b[CHUNK - 1:CHUNK, :]
            mid = 0.5 * b_end
            qt = (qs * jnp.exp(b - mid)).astype(BF16)
            kt = (k * jnp.exp(mid - b)).astype(BF16)
            qb = (qs * jnp.exp(b)).astype(BF16)
            kb = (k * jnp.exp(b_end - b)).astype(BF16)
            dec = jnp.exp(b_end)
            vb = v_ref[rs, co:co + BR_W].astype(BF16)
            for h in range(HG_HEADS):
                sl = slice(h * HG_DK, (h + 1) * HG_DK)
                osl = slice(co + h * HG_DK, co + (h + 1) * HG_DK)
                att = jnp.where(mask, _dot_nt(qt[:, sl], kt[:, sl]), 0.0)
                st = st_ref[bb, h]
                o = _dot(att.astype(BF16), vb[:, sl]) + _dot_nt(qb[:, sl], st.astype(BF16))
                st_ref[bb, h] = dec[:, sl] * st + _dot_tn(vb[:, sl], kb[:, sl])
                if post:
                    o_scr[rs, osl] = o
                else:
                    o_ref[rs, osl] = o
            yield
        if post:
            nrm = nrm_ref[...]
            for h in range(HG_HEADS):
                sl = slice(h * HG_DK, (h + 1) * HG_DK)
                osl = slice(co + h * HG_DK, co + (h + 1) * HG_DK)
                o = of_ref[:, osl] + o_scr[:, osl]
                ms = jnp.mean(o * o, axis=-1, keepdims=True)
                o_ref[:, osl] = (o * lax.rsqrt(ms + EPS)) * nrm[:, sl] * _silu(z_ref[:, osl])


def _hgrn_part(uq, uv, uf, uz, of, st_in, lb, nrm, reverse):
    tp = uq.shape[0]
    tb = min(256, tp)
    nblk = tp // tb
    post = of is not None
    tix = (lambda i: nblk - 1 - i) if reverse else (lambda i: i)
    blk = pl.BlockSpec((tb, BPS * BR_W), lambda b, i: (tix(i), b))
    c2 = lambda b, i: (0, 0)
    stspec = pl.BlockSpec((BPS,) + st_in.shape[1:], lambda b, i: (b, 0, 0, 0))
    in_specs = [blk, blk, blk]
    args = [uq, uv, uf]
    scratch = []
    if post:
        in_specs += [blk, blk, pl.BlockSpec((1, BR_W), c2), pl.BlockSpec((1, BR_W), c2)]
        args += [uz, of, lb, nrm]
        scratch = [pltpu.VMEM((tb, BPS * BR_W), F32)]
    else:
        in_specs += [pl.BlockSpec((1, BR_W), c2)]
        args += [lb]
    in_specs.append(stspec)
    args.append(st_in)
    return dict(
        kern=functools.partial(_hgrn_kernel, reverse=reverse, post=post, tb=tb),
        in_specs=in_specs, args=args, st_in=[a is st_in for a in args].index(True),
        out_shape=[jax.ShapeDtypeStruct((tp, NB * BR_W), F32), jax.ShapeDtypeStruct(st_in.shape, F32)],
        out_specs=[blk, stspec],
        scratch=scratch)


def _ssd_kernel(*refs, reverse, post, tb):
    if post:
        (xs_ref, dt_ref, z_ref, of_ref, dtb_ref, a_ref, e_ref, dsk_ref, nrm_ref,
         st_in_ref, o_ref, st_ref, o_scr) = refs
    else:
        (xs_ref, dt_ref, dtb_ref, a_ref, e_ref, st_in_ref, o_ref, st_ref) = refs

    gn = M2_GROUPS * M2_STATE
    ri = lax.broadcasted_iota(jnp.int32, (CHUNK, BR_W), 0)
    cj = lax.broadcasted_iota(jnp.int32, (CHUNK, BR_W), 1) % CHUNK
    mask = (cj >= ri) if reverse else (cj <= ri)
    teye = (cj == ri).astype(F32)
    r8 = lax.broadcasted_iota(jnp.int32, (BR_W, BR_W), 0) // M2_HEADDIM
    c8 = lax.broadcasted_iota(jnp.int32, (BR_W, BR_W), 1) // M2_HEADDIM
    bdmask = r8 == c8
    hpg = M2_HEADS // M2_GROUPS
    g_row = lax.broadcasted_iota(jnp.int32, (BR_W, gn), 0) // (hpg * CHUNK)
    g_col = lax.broadcasted_iota(jnp.int32, (BR_W, gn), 1) // M2_STATE
    bmask = g_row == g_col
    s_row = lax.broadcasted_iota(jnp.int32, (gn, BR_W), 0) // M2_STATE
    s_col = lax.broadcasted_iota(jnp.int32, (gn, BR_W), 1) // (hpg * M2_HEADDIM)
    smask = s_row == s_col
    zero = jnp.zeros((), BF16)

    nchunk = tb // CHUNK
    order = range(nchunk - 1, -1, -1) if reverse else range(nchunk)
    for bb in range(BPS):
        xo = bb * M2_XBC
        osl = slice(bb * BR_W, (bb + 1) * BR_W)
        dt = _softplus(dt_ref[:, bb * DT_PAD:(bb + 1) * DT_PAD] + dtb_ref[...])
        cum_n = _chunk_cumsum(dt * a_ref[...], reverse)
        ex = _dot(jnp.concatenate(_split3(cum_n) + _split3(dt), axis=0), e_ref[...])
        cum_all = _sum3(ex[0:3 * tb], tb, 0)
        dt_all = _sum3(ex[3 * tb:6 * tb], tb, 0)
        for c in order:
            rs = slice(c * CHUNK, (c + 1) * CHUNK)
            xh = xs_ref[rs, xo:xo + BR_W]
            bm = xs_ref[rs, xo + BR_W:xo + BR_W + gn]
            cm = xs_ref[rs, xo + BR_W + gn:xo + BR_W + 2 * gn]
            cum = cum_all[rs, :]
            dt_e = dt_all[rs, :]
            cum_end = cum[0:1, :] if reverse else cum[CHUNK - 1:CHUNK, :]
            cum_row = jnp.sum(cum * teye, axis=0, keepdims=True)
            dt_row = jnp.sum(dt_e * teye, axis=0, keepdims=True)
            decay = jnp.exp(jnp.where(mask, cum - cum_row, -jnp.inf))
            bm8 = jnp.where(bmask, jnp.concatenate([bm] * M2_HEADS, axis=0), zero)
            scores = _dot_nt(cm, bm8)
            w = (scores * decay * dt_row).astype(BF16)
            xbd = jnp.where(bdmask, jnp.concatenate([xh] * M2_HEADS, axis=0), zero)
            st = st_ref[bb]
            y = _dot(w, xbd) + _dot(cm, st.astype(BF16)) * jnp.exp(cum)
            wx = (xh.astype(F32) * (jnp.exp(cum_end - cum) * dt_e)).astype(BF16)
            st_ref[bb] = jnp.exp(cum_end) * st + jnp.where(smask, _dot_tn(bm, wx), 0.0)
            if post:
                o_scr[rs, osl] = y
            else:
                o_ref[rs, osl] = y
            yield
        if post:
            y = of_ref[:, osl] + o_scr[:, osl] + dsk_ref[...] * xs_ref[:, xo:xo + BR_W].astype(F32)
            yz = y * _silu(z_ref[:, osl])
            ms = jnp.mean(yz * yz, axis=-1, keepdims=True)
            o_ref[:, osl] = (yz * lax.rsqrt(ms + EPS)) * nrm_ref[...]


def _ssd_part(uxs, udt, uz, of, st_in, dtb, a, e, dsk, nrm, reverse):
    tp = uxs.shape[0]
    tb = min(256, tp)
    nblk = tp // tb
    post = of is not None
    gn = M2_GROUPS * M2_STATE
    tix = (lambda i: nblk - 1 - i) if reverse else (lambda i: i)
    tile = lambda w: pl.BlockSpec((tb, BPS * w), lambda b, i: (tix(i), b))
    c2 = lambda b, i: (0, 0)
    stspec = pl.BlockSpec((BPS, gn, BR_W), lambda b, i: (b, 0, 0))
    in_specs = [tile(M2_XBC), tile(DT_PAD)]
    args = [uxs, udt]
    scratch = []
    if post:
        in_specs += [tile(BR_W), tile(BR_W)]
        args += [uz, of]
        scratch = [pltpu.VMEM((tb, BPS * BR_W), F32)]
    in_specs += [pl.BlockSpec((1, DT_PAD), c2), pl.BlockSpec((1, DT_PAD), c2),
                 pl.BlockSpec((DT_PAD, BR_W), c2)]
    args += [dtb, a, e]
    if post:
        in_specs += [pl.BlockSpec((1, BR_W), c2), pl.BlockSpec((1, BR_W), c2)]
        args += [dsk, nrm]
    in_specs.append(stspec)
    args.append(st_in)
    return dict(
        kern=functools.partial(_ssd_kernel, reverse=reverse, post=post, tb=tb),
        in_specs=in_specs, args=args, st_in=[a is st_in for a in args].index(True),
        out_shape=[jax.ShapeDtypeStruct((tp, NB * BR_W), F32), jax.ShapeDtypeStruct(st_in.shape, F32)],
        out_specs=[tile(BR_W), stspec],
        scratch=scratch)


def _s5_params(a_re, a_im, log_step, b_re, b_im, c_re, c_im):
    nl, g, n, p = b_re.shape
    gpc = g // S5_NCH
    m = S5_BLK
    ar, ai = a_re.astype(F32), a_im.astype(F32)
    step = jnp.exp(log_step.astype(F32))[..., None]
    mag = jnp.exp(ar * step)
    ab_re, ab_im = mag * jnp.cos(ai * step), mag * jnp.sin(ai * step)
    den = ar * ar + ai * ai
    co_re = ((ab_re - 1.0) * ar + ab_im * ai) / den
    co_im = (ab_im * ar - (ab_re - 1.0) * ai) / den
    br, bi = b_re.astype(F32), b_im.astype(F32)
    cr, ci = c_re.astype(F32), c_im.astype(F32)
    pr, pi = [jnp.ones_like(ab_re)], [jnp.zeros_like(ab_im)]
    for _ in range(m):
        pr, pi = pr + [pr[-1] * ab_re - pi[-1] * ab_im], pi + [pr[-1] * ab_im + pi[-1] * ab_re]
    pw_re, pw_im = jnp.stack(pr, axis=2), jnp.stack(pi, axis=2)
    steps = np.arange(m)
    e_idx = np.stack([m - 1 - steps, steps])
    g_idx = np.stack([steps + 1, m - steps])
    k_idx = np.abs(steps[:, None] - steps[None, :])
    allow = np.stack([steps[:, None] <= steps[None, :], steps[:, None] >= steps[None, :]]).astype(np.float32)

    def per_dir(pw, idx):
        return jnp.stack([pw[:, d][:, idx[d]] for d in range(2)], axis=1)

    ch = lambda t, ax: t.reshape(t.shape[:ax] + (S5_NCH, gpc) + t.shape[ax + 1:])
    hi = lax.Precision.HIGHEST

    def block_diag(t, rows, cols):
        q = np.arange(m * gpc * p)
        if cols == "n":
            tmat = np.tile(np.eye(n, dtype=np.float32), (1, gpc))
            col_g = np.arange(gpc * n) // n
        else:
            tmat = ((q[None, :] // (gpc * p) == np.arange(m * p)[:, None] // p)
                    & (q[None, :] % p == np.arange(m * p)[:, None] % p)).astype(np.float32)
            col_g = (q // p) % gpc
        r = np.arange(t.shape[3])
        row_g = (r // n) if rows == "gn" else (r // p) % gpc
        mask = (row_g[:, None] == col_g[None, :]).astype(np.float32)
        return jnp.einsum('ldcrk,kq->ldcrq', t, tmat, precision=hi) * mask

    er, ei = ch(per_dir(pw_re, e_idx), 3), ch(per_dir(pw_im, e_idx), 3)
    brc, bic = ch(br, 1)[:, None, None], ch(bi, 1)[:, None, None]
    rows_igp = lambda t: jnp.transpose(t, (0, 1, 3, 2, 4, 6, 5)).reshape(nl, 2, S5_NCH, m * gpc * p, n)
    wd = jnp.concatenate([block_diag(rows_igp(er[..., None] * brc - ei[..., None] * bic), "igp", "n"),
                          block_diag(rows_igp(er[..., None] * bic + ei[..., None] * brc), "igp", "n")],
                         axis=-1).astype(BF16)
    cc_re = cr[:, None] * co_re[:, :, :, None, :] - ci[:, None] * co_im[:, :, :, None, :]
    cc_im = cr[:, None] * co_im[:, :, :, None, :] + ci[:, None] * co_re[:, :, :, None, :]
    gr, gi = per_dir(pw_re, g_idx)[:, :, :, :, None, :], per_dir(pw_im, g_idx)[:, :, :, :, None, :]
    rows_gn = lambda t: jnp.transpose(ch(t, 3), (0, 1, 3, 4, 6, 2, 5)).reshape(nl, 2, S5_NCH, gpc * n, m * p)
    wy = jnp.concatenate([block_diag(rows_gn(cc_re[:, :, None] * gr - cc_im[:, :, None] * gi), "gn", "p"),
                          block_diag(rows_gn(-(cc_re[:, :, None] * gi + cc_im[:, :, None] * gr)), "gn", "p")],
                         axis=-2).astype(BF16)
    kr, ki = pw_re[:, :, 0:m, :, None, :], pw_im[:, :, 0:m, :, None, :]
    car = cc_re[:, :, None] * kr - cc_im[:, :, None] * ki
    cai = cc_re[:, :, None] * ki + cc_im[:, :, None] * kr
    thru = (jnp.einsum('ldkgon,lgni->ldkgio', car, br, precision=hi)
            - jnp.einsum('ldkgon,lgni->ldkgio', cai, bi, precision=hi))
    tij = thru[:, :, k_idx] * allow[None, :, :, :, None, None, None]
    wf = block_diag(jnp.transpose(ch(tij, 4), (0, 1, 4, 2, 5, 6, 3, 7)).reshape(
        nl, 2, S5_NCH, m * gpc * p, m * p), "igp", "p").astype(BF16)
    ab = jnp.concatenate([pw_re[:, :, m].reshape(nl, 2, S5_NCH, gpc * n),
                          pw_im[:, :, m].reshape(nl, 2, S5_NCH, gpc * n)], axis=-1)
    ab = jnp.broadcast_to(ab[:, :, :, None, :], (nl, 2, S5_NCH, NB, 2 * gpc * n))
    return wd, ab, wy, wf


def _lane_vec(v, d):
    out = jnp.zeros((1, DT_PAD), F32)
    return lax.dynamic_update_slice(out, v.astype(F32)[None, :], (0, M2_HEADS * d))


def _head_expand(d):
    r = jnp.arange(DT_PAD)[:, None]
    c = jnp.arange(BR_W)[None, :] // M2_HEADDIM
    return (r == c + M2_HEADS * d).astype(BF16)


def kernel(x, c, ctx, c_ctx, norm_w, w_mod, b_mod, w_in, hg_lb_logits, hg_norm, s5_a_re, s5_a_im, s5_log_step, s5_b_re, s5_b_im, s5_c_re, s5_c_im, s5_d, s5_w_glu, s5_b_glu, lru_conv_w, lru_conv_b, lru_gate_w, lru_gate_b, lru_lam, m2_conv_w, m2_conv_b, m2_dt_bias, m2_a_log, m2_d, m2_norm, w_branch, w_gate, b_gate, w_out, final_norm):
    bsz, seq, dm = x.shape
    n_ctx = ctx.shape[1]
    depth = norm_w.shape[0]
    assert bsz == NB and seq % 256 == 0 and n_ctx % CHUNK == 0

    xl, xc = x, ctx
    c_all = jnp.concatenate([c, jnp.broadcast_to(c_ctx[None, :], (NB, dm))], axis=0)
    lb_all = jnp.cumsum(jax.nn.softmax(hg_lb_logits.astype(F32), axis=0), axis=0)
    perm = _tile_perm()
    perm_t = perm.T

    o_s5 = 5 * BR_W
    o_lru = o_s5 + 2 * BR_W
    o_lz = o_lru + BR_W
    o_m2 = o_lru + 2 * BR_W
    o_dt = o_m2 + M2_XBC
    o_mz = o_dt + 2 * M2_HEADS
    gn = M2_GROUPS * M2_STATE

    s5_w = _s5_params(s5_a_re, s5_a_im, s5_log_step, s5_b_re, s5_b_im, s5_c_re, s5_c_im)
    nbh = lru_gate_w.shape[3] // LRU_HALVES
    bw = lru_gate_w.shape[-1]
    gw = (0.5 * lru_gate_w).reshape(depth, 2, 2, LRU_HALVES, nbh, bw, bw)
    lru_wg = jnp.einsum('ldgjnab,nm->ldjnagmb', gw, jnp.eye(nbh, dtype=F32)).reshape(
        depth, 2, LRU_HALVES, nbh * bw, 2 * nbh * bw).astype(BF16)
    lru_bg = jnp.transpose((0.5 * lru_gate_b).reshape(depth, 2, 2, LRU_HALVES, nbh * bw),
                           (0, 1, 3, 2, 4)).reshape(depth, 2, LRU_HALVES, 1, 2 * nbh * bw)

    for l in range(depth):
        mod = _mod_call(c_all, w_mod[l].astype(BF16), b_mod[l][None, :])
        nw = norm_w[l][None, :]
        wl = w_in[l]
        w_s5 = wl[:, o_s5:o_lru].astype(BF16)
        w_lx = wl[:, o_lru:o_lz].astype(BF16)
        w_lz = wl[:, o_lz:o_m2].astype(BF16)
        w_mx = wl[:, o_m2:o_dt].astype(BF16)
        w_bm = jnp.concatenate(
            [wl[:, 0:o_s5], jnp.pad(wl[:, o_dt:o_mz], ((0, 0), (0, DT_PAD - 2 * M2_HEADS))), wl[:, o_mz:]],
            axis=1).astype(BF16)
        wg = w_gate[l].astype(BF16)
        bg = b_gate[l][:, None, :]
        wb = w_branch[l].astype(BF16)
        wo = w_out[l].astype(BF16)

        parts = []
        for name, x3, row0, period in (("ctx", xc, NB, n_ctx), ("lat", xl, 0, CHUNK)):
            sh = mod[row0:row0 + NB, 0:dm]
            sc = mod[row0:row0 + NB, dm:2 * dm]
            gt = mod[row0:row0 + NB, 2 * dm:3 * dm]
            ub, ucx, ucz, uxs, uq, uv, uff, ufb, uaz, udt, udz = _proj_call(
                x3, nw, sc, sh, perm, perm_t, w_s5, w_lx, w_lz, lru_conv_w[l], lru_conv_b[l][None, :],
                w_mx, w_bm, (BR_W,) * 5 + (DT_PAD, BR_W), m2_conv_w[l], m2_conv_b[l][None, :],
                period // CHUNK, "proj_" + name)
            parts.append(dict(name=name, x=x3, sc=sc, sh=sh, gt=gt, uq=uq, uv=uv, uf=(uff, ufb), uaz=uaz,
                              ub=ub, ucx=ucx, ucz=ucz, uxs=uxs, udt=udt, udz=udz))

        ys = [dict(), dict()]
        for d, reverse in ((0, False), (1, True)):
            tag = "bwd" if reverse else "fwd"
            st_a = jnp.zeros((NB, HG_HEADS, HG_DK, HG_DK), F32)
            st_b = jnp.zeros((NB, 2 * (BR_W // S5_GROUP) * S5_STATE), F32)
            st_c = jnp.zeros((NB, BR_W), F32)
            st_d = jnp.zeros((NB, gn, BR_W), F32)
            for pi, p in enumerate(parts):
                nm = tag + "_" + p["name"]
                of = ys[pi] if reverse else dict(a=None, b=None, c=None, d=None)
                r = p["ub"].shape[0]
                tp = r // NB
                (ob, st_b), (oc, st_c) = _fused_call(
                    [_s5_part(p["ub"], of["b"], st_b, *s5_w, (l, d), s5_d[l][None, :],
                              s5_w_glu[l].astype(BF16), s5_b_glu[l][None, :], reverse),
                     _lru_part(p["ucx"], p["ucz"], of["c"], st_c, lru_wg, lru_bg, (l, d),
                               lru_lam[l, d][None, :], reverse)],
                    (r // TILE,), "tm_" + nm)
                (oa, st_a), (od, st_d) = _fused_call(
                    [_hgrn_part(p["uq"], p["uv"], p["uf"][d], p["uaz"], of["a"], st_a,
                                lb_all[l, d][None, :], hg_norm[l][None, :], reverse),
                     _ssd_part(p["uxs"], p["udt"], p["udz"], of["d"], st_d,
                               _lane_vec(m2_dt_bias[l, d], d),
                               _lane_vec(-jnp.exp(m2_a_log[l, d].astype(F32)), d), _head_expand(d),
                               jnp.repeat(m2_d[l].astype(F32), M2_HEADDIM)[None, :],
                               m2_norm[l][None, :], reverse)],
                    (NB // BPS, tp // min(256, tp)), "bm_" + nm)
                ys[pi] = dict(a=oa, b=ob, c=oc, d=od)

        last = l == depth - 1
        for pi, p in enumerate(parts):
            if last and p["name"] == "ctx":
                continue
            y = ys[pi]
            out = _merge_call(p["x"], (y["a"], y["b"], y["c"], y["d"]), nw, p["sc"], p["sh"], p["gt"],
                              perm, wg, bg, wb, wo, final_norm[None, :], last, "merge_" + p["name"])
            if p["name"] == "ctx":
                xc = out
            else:
                xl = out
    return xl
```

```python
import functools
import math

import jax
import jax.numpy as jnp
import numpy as np
from jax import lax
from jax.experimental import pallas as pl
from jax.experimental.pallas import tpu as pltpu

F32 = jnp.float32
BF16 = jnp.bfloat16

NB = 8
EPS = 1e-6
CHUNK = 64
TILE = CHUNK * NB
HALO = 2 * NB
CONV_W = 4
BR_W = 512
HG_HEADS = 4
HG_DK = BR_W // HG_HEADS
S5_GROUP = 16
S5_STATE = 64
S5_NCH = 8
S5_BLK = 4
LRU_C = 8.0
LRU_HALVES = 2
M2_HEADDIM = 64
M2_HEADS = BR_W // M2_HEADDIM
M2_GROUPS = 2
M2_STATE = 64
M2_XBC = BR_W + 2 * M2_GROUPS * M2_STATE
DT_PAD = 128
BPS = 4
VMEM_LIMIT = 56 * 1024 * 1024


def _cp(n_axes):
    return pltpu.CompilerParams(dimension_semantics=("arbitrary",) * n_axes,
                                vmem_limit_bytes=VMEM_LIMIT)


def _const_spec(shape):
    nd = len(shape)
    return pl.BlockSpec(shape, lambda *_: (0,) * nd, pipeline_mode=pl.Buffered(1))


def _dot(a, b):
    return jnp.dot(a, b, preferred_element_type=F32)


def _dot_nt(a, b):
    return lax.dot_general(a, b, (((1,), (1,)), ((), ())), preferred_element_type=F32)


def _dot_tn(a, b):
    return lax.dot_general(a, b, (((0,), (0,)), ((), ())), preferred_element_type=F32)


def _split3(x):
    hi = x.astype(BF16)
    r1 = x - hi.astype(F32)
    mid = r1.astype(BF16)
    lo = (r1 - mid.astype(F32)).astype(BF16)
    return hi, mid, lo


def _sum3(y, n, axis):
    if axis == 1:
        return (y[:, 0:n] + y[:, n:2 * n]) + y[:, 2 * n:3 * n]
    return (y[0:n] + y[n:2 * n]) + y[2 * n:3 * n]


def _chunk_cumsum(x, reverse):
    n = x.shape[0]
    i = lax.broadcasted_iota(jnp.int32, (n, n), 0)
    j = lax.broadcasted_iota(jnp.int32, (n, n), 1)
    tri = ((i // CHUNK == j // CHUNK) & ((j >= i) if reverse else (j <= i))).astype(BF16)
    y = _dot(tri, jnp.concatenate(_split3(x), axis=1))
    return _sum3(y, x.shape[1], 1)


def _sigmoid(x):
    return 0.5 * jnp.tanh(0.5 * x) + 0.5


def _silu(x):
    return x * _sigmoid(x)


def _softplus(x):
    return jnp.maximum(x, 0.0) + jnp.log1p(jnp.exp(-jnp.abs(x)))


def _adaln(x3, nw, sc, sh):
    nb, n, d = x3.shape
    ms = jnp.mean(x3 * x3, axis=-1, keepdims=True)
    xn = (x3 * lax.rsqrt(ms + EPS)) * nw
    h = xn * (1.0 + sc)[:, None, :] + sh[:, None, :]
    return h.reshape(nb * n, d)


def _tile_perm():
    i = jnp.arange(TILE)
    src = (i % CHUNK) * NB + i // CHUNK
    return (src[:, None] == jnp.arange(TILE)[None, :]).astype(BF16)


def _halo_select(first_step):
    i = jnp.arange(HALO)
    src = (i % NB) * NB + first_step + i // NB
    return (src[:, None] == jnp.arange(NB * NB)[None, :]).astype(BF16)


def _mod_kernel(c_ref, w_ref, b_ref, o_ref):
    s = _silu(c_ref[...]).astype(BF16)
    o_ref[...] = _dot(s, w_ref[...]) + b_ref[...]


def _mod_call(c_all, w, b):
    n, d = c_all.shape
    m = w.shape[1]
    return pl.pallas_call(
        _mod_kernel,
        out_shape=jax.ShapeDtypeStruct((n, m), F32),
        grid=(1,),
        in_specs=[pl.BlockSpec((n, d), lambda i: (0, 0)),
                  pl.BlockSpec((d, m), lambda i: (0, 0)),
                  pl.BlockSpec((1, m), lambda i: (0, 0))],
        out_specs=pl.BlockSpec((n, m), lambda i: (0, 0)),
        compiler_params=_cp(1),
        name="mod",
    )(c_all, w, b)


def _conv_tm(hb, halo, w_ref, cw_ref, cb_ref, ptiles):
    u = _dot(hb, w_ref[...])
    width = u.shape[1]
    if ptiles == 1:
        up = jnp.zeros((NB, width), F32)
        un = jnp.zeros((HALO, width), F32)
    else:
        i = pl.program_id(0)
        vp = (i % ptiles != 0).astype(F32)
        vn = (i % ptiles != ptiles - 1).astype(F32)
        up = _dot(halo[0], w_ref[...])[NB:HALO, :] * vp
        un = _dot(halo[1], w_ref[...]) * vn
    ue = jnp.concatenate([up, u, un], axis=0)
    xc = cb_ref[...] + cw_ref[0:1, :] * ue[0:TILE, :]
    for k in range(1, CONV_W):
        xc = xc + cw_ref[k:k + 1, :] * ue[k * NB:k * NB + TILE, :]
    return xc


def _store_bm(o_ref, u, wd, c0):
    cw = u.shape[1]
    for b in range(NB):
        o_ref[:, b * wd + c0:b * wd + c0 + cw] = u[b * CHUNK:(b + 1) * CHUNK, :].astype(o_ref.dtype)


def _proj_kernel(*refs, ptiles, widths):
    if ptiles == 1:
        x_ref, nw_ref, sc_ref, sh_ref, p_ref, pt_ref = refs[:6]
        rest = refs[6:]
        halo = None
    else:
        x_ref, xp_ref, xn_ref, nw_ref, sc_ref, sh_ref, p_ref, pt_ref, sp_ref, sn_ref = refs[:10]
        rest = refs[10:]
        halo = tuple(
            _dot(s_ref[...], _adaln(r[...], nw_ref[...], sc_ref[...], sh_ref[...]).astype(BF16)).astype(BF16)
            for s_ref, r in ((sp_ref, xp_ref), (sn_ref, xn_ref)))
    (w_s5_ref, w_lx_ref, w_lz_ref, cwl_ref, cbl_ref, w_x_ref, w_ref, cwm_ref, cbm_ref,
     ub_ref, ucx_ref, ucz_ref, uxs_ref) = rest[:13]
    o_refs = rest[13:]
    hb_bm = _adaln(x_ref[...], nw_ref[...], sc_ref[...], sh_ref[...]).astype(BF16)
    hb_tm = _dot(pt_ref[...], hb_bm).astype(BF16)
    ub_ref[...] = _dot(hb_tm, w_s5_ref[...])
    ucx_ref[...] = _conv_tm(hb_tm, halo, w_lx_ref, cwl_ref, cbl_ref, ptiles)
    ucz_ref[...] = _dot(hb_tm, w_lz_ref[...])
    xs = _silu(_conv_tm(hb_tm, halo, w_x_ref, cwm_ref, cbm_ref, ptiles)).astype(BF16)
    _store_bm(uxs_ref, _dot(p_ref[...], xs), M2_XBC, 0)
    off = 0
    for o_ref, wd in zip(o_refs, widths):
        for c0 in range(0, wd, BR_W):
            cw = min(BR_W, wd - c0)
            _store_bm(o_ref, _dot(hb_bm, w_ref[:, off + c0:off + c0 + cw]), wd, c0)
        off += wd


def _proj_call(x3, nw, sc, sh, perm, perm_t, w_s5, w_lx, w_lz, cwl, cbl, w_x, w, widths, cwm, cbm,
               ptiles, name):
    nb, t, d = x3.shape
    r = nb * t
    blk = lambda i: (0, i, 0)
    row = lambda i: (i, 0)
    in_specs = [pl.BlockSpec((NB, CHUNK, d), blk)]
    args = [x3]
    if ptiles > 1:
        per = CHUNK // NB
        last = t // NB - 1
        in_specs += [pl.BlockSpec((NB, NB, d), lambda i: (0, jnp.maximum(i * per - 1, 0), 0)),
                     pl.BlockSpec((NB, NB, d), lambda i: (0, jnp.minimum((i + 1) * per, last), 0))]
        args += [x3, x3]
    consts = [nw, sc, sh, perm, perm_t]
    if ptiles > 1:
        consts += [_halo_select(NB - 2), _halo_select(0)]
    consts += [w_s5, w_lx, w_lz, cwl, cbl, w_x, w, cwm, cbm]
    for a in consts:
        in_specs.append(_const_spec(a.shape))
        args.append(a)
    tm_w = (2 * BR_W, BR_W, BR_W)
    bm = [(M2_XBC, BF16)] + [(wd, F32) for wd in widths]
    return pl.pallas_call(
        functools.partial(_proj_kernel, ptiles=ptiles, widths=widths),
        out_shape=[jax.ShapeDtypeStruct((r, wd), F32) for wd in tm_w]
                  + [jax.ShapeDtypeStruct((t, NB * wd), dt) for wd, dt in bm],
        grid=(t // CHUNK,),
        in_specs=in_specs,
        out_specs=[pl.BlockSpec((TILE, wd), row) for wd in tm_w]
                  + [pl.BlockSpec((CHUNK, NB * wd), row) for wd, _ in bm],
        compiler_params=_cp(1),
        name=name,
    )(*args)


def _merge_kernel(x_ref, ya_ref, yb_ref, yc_ref, yd_ref, nw_ref, sc_ref, sh_ref, gt_ref, p_ref,
                  wg_ref, bg_ref, wb_ref, wo_ref, fn_ref, o_ref, *, final):
    x3 = x_ref[...]
    hb = _adaln(x3, nw_ref[...], sc_ref[...], sh_ref[...]).astype(BF16)

    def from_bm(y_ref):
        return jnp.concatenate([y_ref[:, b * BR_W:(b + 1) * BR_W] for b in range(NB)],
                               axis=0).astype(BF16)

    def from_tm(y_ref):
        return _dot(p_ref[...], y_ref[...].astype(BF16)).astype(BF16)

    ys = (from_bm(ya_ref), from_tm(yb_ref), from_tm(yc_ref), from_bm(yd_ref))
    m = None
    for k, y in enumerate(ys):
        g = _sigmoid(_dot(hb, wg_ref[k]) + bg_ref[k])
        p = _dot(y, wb_ref[k])
        m = g * p if m is None else m + g * p
    upd = _dot(m.astype(BF16), wo_ref[...])
    out = x3 + upd.reshape(x3.shape) * gt_ref[...][:, None, :]
    if final:
        ms = jnp.mean(out * out, axis=-1, keepdims=True)
        out = (out * lax.rsqrt(ms + EPS)) * fn_ref[...]
    o_ref[...] = out


def _merge_call(x3, ys, nw, sc, sh, gt, perm, wg, bg, wb, wo, fn, final, name):
    nb, t, d = x3.shape
    blk = lambda i: (0, i, 0)
    row = lambda i: (i, 0)
    tm_spec = pl.BlockSpec((TILE, BR_W), row)
    bm_spec = pl.BlockSpec((CHUNK, NB * BR_W), row)
    return pl.pallas_call(
        functools.partial(_merge_kernel, final=final),
        out_shape=jax.ShapeDtypeStruct(x3.shape, F32),
        grid=(t // CHUNK,),
        in_specs=[pl.BlockSpec((NB, CHUNK, d), blk), bm_spec, tm_spec, tm_spec, bm_spec,
                  _const_spec((1, d)), _const_spec((NB, d)), _const_spec((NB, d)),
                  _const_spec((NB, d)), _const_spec((TILE, TILE)),
                  _const_spec(wg.shape), _const_spec(bg.shape), _const_spec(wb.shape),
                  _const_spec(wo.shape), _const_spec((1, d))],
        out_specs=pl.BlockSpec((NB, CHUNK, d), blk),
        compiler_params=_cp(1),
        name=name,
    )(x3, *ys, nw, sc, sh, gt, perm, wg, bg, wb, wo, fn)


def _fused_call(parts, grid, name):
    n_in = [len(p["args"]) for p in parts]
    n_out = [len(p["out_shape"]) for p in parts]
    n_scr = [len(p["scratch"]) for p in parts]

    def kern(*refs):
        ins = refs[:sum(n_in)]
        outs = refs[sum(n_in):sum(n_in) + sum(n_out)]
        scr = refs[sum(n_in) + sum(n_out):]
        split = []
        i = o = s = 0
        for a, b, c in zip(n_in, n_out, n_scr):
            split.append((ins[i:i + a], outs[o:o + b], scr[s:s + c]))
            i, o, s = i + a, o + b, s + c

        @pl.when(pl.program_id(len(grid) - 1) == 0)
        def _():
            for p, (pin, pout, _) in zip(parts, split):
                pout[1][...] = pin[p["st_in"]][...]

        active = [p["kern"](*pin, *pout, *pscr) for p, (pin, pout, pscr) in zip(parts, split)]
        while active:
            for g in list(active):
                if next(g, StopIteration) is StopIteration:
                    active.remove(g)

    res = pl.pallas_call(
        kern,
        out_shape=[x for p in parts for x in p["out_shape"]],
        grid=grid,
        in_specs=[x for p in parts for x in p["in_specs"]],
        out_specs=[x for p in parts for x in p["out_specs"]],
        scratch_shapes=[x for p in parts for x in p["scratch"]],
        compiler_params=_cp(len(grid)),
        name=name,
    )(*[x for p in parts for x in p["args"]])
    out, o = [], 0
    for b in n_out:
        out.append(res[o:o + b])
        o += b
    return out


def _lru_kernel(*refs, reverse, post):
    if post:
        (x_ref, z_ref, of_ref, st_in_ref, wg_ref, bg_ref, lam_ref, o_ref, st_ref, a_scr, b_scr) = refs
    else:
        (x_ref, st_in_ref, wg_ref, bg_ref, lam_ref, o_ref, st_ref, a_scr, b_scr) = refs

    c1 = (-0.5 * LRU_C) * _softplus(-lam_ref[...])
    nq = 4
    qr = TILE // nq
    hw = BR_W // LRU_HALVES
    for q in (range(nq - 1, -1, -1) if reverse else range(nq)):
        rs = slice(q * qr, (q + 1) * qr)
        for j in range(LRU_HALVES):
            cs = slice(j * hw, (j + 1) * hw)
            xc = x_ref[rs, cs]
            th = jnp.tanh(_dot(xc.astype(BF16), wg_ref[j]) + bg_ref[j])
            log_a = c1[:, cs] * th[:, 0:hw] + c1[:, cs]
            a = jnp.exp(log_a)
            a_scr[rs, cs] = a
            v = -jnp.tanh(log_a) * (a * a + 1.0)
            root = jnp.where(v > 0.0, v * lax.rsqrt(v), 0.0)
            b_scr[rs, cs] = root * ((0.5 * th[:, hw:2 * hw] + 0.5) * xc)
        yield
    h = st_ref[...]
    for i in range(CHUNK):
        r0 = ((CHUNK - 1 - i) if reverse else i) * NB
        h = a_scr[r0:r0 + NB, :] * h + b_scr[r0:r0 + NB, :]
        b_scr[r0:r0 + NB, :] = h
        yield
    st_ref[...] = h
    if post:
        o_ref[...] = (of_ref[...] + b_scr[...]) * _silu(z_ref[...])
    else:
        o_ref[...] = b_scr[...]


def _pick_spec(a, lead):
    k = len(lead)
    return pl.BlockSpec((None,) * k + a.shape[k:], lambda *_: tuple(lead) + (0,) * (a.ndim - k))


def _lru_part(xc, z, of, st_in, wg, bg, ld, lam, reverse):
    r = xc.shape[0]
    nblk = r // TILE
    post = of is not None
    tmap = (lambda i: (nblk - 1 - i, 0)) if reverse else (lambda i: (i, 0))
    c2 = lambda i: (0, 0)
    tile = pl.BlockSpec((TILE, BR_W), tmap)
    in_specs = [tile]
    args = [xc]
    if post:
        in_specs += [tile, tile]
        args += [z, of]
    in_specs += [pl.BlockSpec((NB, BR_W), c2), _pick_spec(wg, ld), _pick_spec(bg, ld),
                 pl.BlockSpec((1, BR_W), c2)]
    args += [st_in, wg, bg, lam]
    return dict(
        kern=functools.partial(_lru_kernel, reverse=reverse, post=post),
        in_specs=in_specs, args=args, st_in=[a is st_in for a in args].index(True),
        out_shape=[jax.ShapeDtypeStruct((r, BR_W), F32), jax.ShapeDtypeStruct((NB, BR_W), F32)],
        out_specs=[tile, pl.BlockSpec((NB, BR_W), c2)],
        scratch=[pltpu.VMEM((TILE, BR_W), F32), pltpu.VMEM((TILE, BR_W), F32)])


def _gelu_tanh(x):
    return 0.5 * x * (1.0 + jnp.tanh(math.sqrt(2.0 / math.pi) * (x + 0.044715 * (x * x * x))))


def _s5_kernel(*refs, reverse, post):
    if post:
        (u_ref, of_ref, st_in_ref, wd_ref, ab_ref, wy_ref, wf_ref, dsk_ref, wglu_ref, bglu_ref,
         o_ref, st_ref, s_scr, y_scr) = refs
    else:
        (u_ref, st_in_ref, wd_ref, ab_ref, wy_ref, wf_ref, o_ref, st_ref, s_scr) = refs
        y_scr = o_ref

    half = (BR_W // S5_GROUP) * S5_STATE // S5_NCH
    cin = BR_W // S5_NCH
    nk = CHUNK // S5_BLK
    u4s = []
    for c in range(S5_NCH):
        ug = u_ref[:, c * cin:(c + 1) * cin].reshape(nk, S5_BLK, NB, cin)
        u4 = jnp.concatenate([ug[:, i].reshape(nk * NB, cin) for i in range(S5_BLK)], axis=1).astype(BF16)
        u4s.append(u4)
        s_scr[c] = _dot(u4, wd_ref[c])
        yield
    for c in range(S5_NCH):
        a_re = ab_ref[c, :, 0:half]
        a_im = ab_ref[c, :, half:2 * half]
        c0 = c * 2 * half
        s_re = st_ref[:, c0:c0 + half]
        s_im = st_ref[:, c0 + half:c0 + 2 * half]
        for kk in range(nk):
            r0 = ((nk - 1 - kk) if reverse else kk) * NB
            n_re = a_re * s_re - a_im * s_im + s_scr[c, r0:r0 + NB, 0:half]
            n_im = a_re * s_im + a_im * s_re + s_scr[c, r0:r0 + NB, half:2 * half]
            s_scr[c, r0:r0 + NB, 0:half] = s_re
            s_scr[c, r0:r0 + NB, half:2 * half] = s_im
            s_re, s_im = n_re, n_im
            yield
        st_ref[:, c0:c0 + half] = s_re
        st_ref[:, c0 + half:c0 + 2 * half] = s_im
        y4 = _dot(s_scr[c].astype(BF16), wy_ref[c]) + _dot(u4s[c], wf_ref[c])
        y = jnp.stack([y4[:, i * cin:(i + 1) * cin].reshape(nk, NB, cin) for i in range(S5_BLK)], axis=1)
        y_scr[:, c * cin:(c + 1) * cin] = y.reshape(TILE, cin)
        yield
    if post:
        u = u_ref[:, 0:BR_W]
        z = u_ref[:, BR_W:2 * BR_W]
        y = of_ref[...] + y_scr[...] + dsk_ref[...] * u
        g = _gelu_tanh(y)
        gl = _dot(g.astype(BF16), wglu_ref[...]) + bglu_ref[...]
        o_ref[...] = g * _sigmoid(gl) * _silu(z)


def _s5_part(u, of, st_in, wd, ab, wy, wf, ld, dsk, wglu, bglu, reverse):
    r = u.shape[0]
    nblk = r // TILE
    post = of is not None
    nst = st_in.shape[1]
    tmap = (lambda i: (nblk - 1 - i, 0)) if reverse else (lambda i: (i, 0))
    c2 = lambda i: (0, 0)
    in_specs = [pl.BlockSpec((TILE, 2 * BR_W), tmap)]
    args = [u]
    if post:
        in_specs.append(pl.BlockSpec((TILE, BR_W), tmap))
        args.append(of)
    in_specs += [pl.BlockSpec((NB, nst), c2)] + [_pick_spec(a, ld) for a in (wd, ab, wy, wf)]
    args += [st_in, wd, ab, wy, wf]
    scratch = [pltpu.VMEM((S5_NCH, TILE // S5_BLK, nst // S5_NCH), F32)]
    if post:
        in_specs += [pl.BlockSpec((1, BR_W), c2), pl.BlockSpec((BR_W, BR_W), c2),
                     pl.BlockSpec((1, BR_W), c2)]
        args += [dsk, wglu, bglu]
        scratch.append(pltpu.VMEM((TILE, BR_W), F32))
    return dict(
        kern=functools.partial(_s5_kernel, reverse=reverse, post=post),
        in_specs=in_specs, args=args, st_in=[a is st_in for a in args].index(True),
        out_shape=[jax.ShapeDtypeStruct((r, BR_W), F32), jax.ShapeDtypeStruct((NB, nst), F32)],
        out_specs=[pl.BlockSpec((TILE, BR_W), tmap), pl.BlockSpec((NB, nst), c2)],
        scratch=scratch)


def _tri_mask(n, reverse):
    i = lax.broadcasted_iota(jnp.int32, (n, n), 0)
    j = lax.broadcasted_iota(jnp.int32, (n, n), 1)
    return (j >= i) if reverse else (j <= i)


def _hgrn_kernel(*refs, reverse, post, tb):
    if post:
        (q_ref, v_ref, f_ref, z_ref, of_ref, lb_ref, nrm_ref, st_in_ref,
         o_ref, st_ref, o_scr) = refs
    else:
        (q_ref, v_ref, f_ref, lb_ref, st_in_ref, o_ref, st_ref) = refs

    mask = _tri_mask(CHUNK, reverse)
    lb = lb_ref[...]
    nchunk = tb // CHUNK
    order = range(nchunk - 1, -1, -1) if reverse else range(nchunk)
    for bb in range(BPS):
        co = bb * BR_W
        f_all = lb + (1.0 - lb) * _sigmoid(f_ref[:, co:co + BR_W])
        b_all = _chunk_cumsum(jnp.log(f_all), reverse)
        for c in order:
            rs = slice(c * CHUNK, (c + 1) * CHUNK)
            k = 1.0 - f_all[rs, :]
            qs = _silu(q_ref[rs, co:co + BR_W])
            b = b_all[rs, :]
            b_end = b[0:1, :] if reverse else b[CHUNK - 1:CHUNK, :]
            mid = 0.5 * b_end
            qt = (qs * jnp.exp(b - mid)).astype(BF16)
            kt = (k * jnp.exp(mid - b)).astype(BF16)
            qb = (qs * jnp.exp(b)).astype(BF16)
            kb = (k * jnp.exp(b_end - b)).astype(BF16)
            dec = jnp.exp(b_end)
            vb = v_ref[rs, co:co + BR_W].astype(BF16)
            for h in range(HG_HEADS):
                sl = slice(h * HG_DK, (h + 1) * HG_DK)
                osl = slice(co + h * HG_DK, co + (h + 1) * HG_DK)
                att = jnp.where(mask, _dot_nt(qt[:, sl], kt[:, sl]), 0.0)
                st = st_ref[bb, h]
                o = _dot(att.astype(BF16), vb[:, sl]) + _dot_nt(qb[:, sl], st.astype(BF16))
                st_ref[bb, h] = dec[:, sl] * st + _dot_tn(vb[:, sl], kb[:, sl])
                if post:
                    o_scr[rs, osl] = o
                else:
                    o_ref[rs, osl] = o
            yield
        if post:
            nrm = nrm_ref[...]
            for h in range(HG_HEADS):
                sl = slice(h * HG_DK, (h + 1) * HG_DK)
                osl = slice(co + h * HG_DK, co + (h + 1) * HG_DK)
                o = of_ref[:, osl] + o_scr[:, osl]
                ms = jnp.mean(o * o, axis=-1, keepdims=True)
                o_ref[:, osl] = (o * lax.rsqrt(ms + EPS)) * nrm[:, sl] * _silu(z_ref[:, osl])


def _hgrn_part(uq, uv, uf, uz, of, st_in, lb, nrm, reverse):
    tp = uq.shape[0]
    tb = min(256, tp)
    nblk = tp // tb
    post = of is not None
    tix = (lambda i: nblk - 1 - i) if reverse else (lambda i: i)
    blk = pl.BlockSpec((tb, BPS * BR_W), lambda b, i: (tix(i), b))
    c2 = lambda b, i: (0, 0)
    stspec = pl.BlockSpec((BPS,) + st_in.shape[1:], lambda b, i: (b, 0, 0, 0))
    in_specs = [blk, blk, blk]
    args = [uq, uv, uf]
    scratch = []
    if post:
        in_specs += [blk, blk, pl.BlockSpec((1, BR_W), c2), pl.BlockSpec((1, BR_W), c2)]
        args += [uz, of, lb, nrm]
        scratch = [pltpu.VMEM((tb, BPS * BR_W), F32)]
    else:
        in_specs += [pl.BlockSpec((1, BR_W), c2)]
        args += [lb]
    in_specs.append(stspec)
    args.append(st_in)
    return dict(
        kern=functools.partial(_hgrn_kernel, reverse=reverse, post=post, tb=tb),
        in_specs=in_specs, args=args, st_in=[a is st_in for a in args].index(True),
        out_shape=[jax.ShapeDtypeStruct((tp, NB * BR_W), F32), jax.ShapeDtypeStruct(st_in.shape, F32)],
        out_specs=[blk, stspec],
        scratch=scratch)


def _ssd_kernel(*refs, reverse, post, tb):
    if post:
        (xs_ref, dt_ref, z_ref, of_ref, dtb_ref, a_ref, e_ref, dsk_ref, nrm_ref,
         st_in_ref, o_ref, st_ref, o_scr) = refs
    else:
        (xs_ref, dt_ref, dtb_ref, a_ref, e_ref, st_in_ref, o_ref, st_ref) = refs

    gn = M2_GROUPS * M2_STATE
    ri = lax.broadcasted_iota(jnp.int32, (CHUNK, BR_W), 0)
    cj = lax.broadcasted_iota(jnp.int32, (CHUNK, BR_W), 1) % CHUNK
    mask = (cj >= ri) if reverse else (cj <= ri)
    teye = (cj == ri).astype(F32)
    r8 = lax.broadcasted_iota(jnp.int32, (BR_W, BR_W), 0) // M2_HEADDIM
    c8 = lax.broadcasted_iota(jnp.int32, (BR_W, BR_W), 1) // M2_HEADDIM
    bdmask = r8 == c8
    hpg = M2_HEADS // M2_GROUPS
    g_row = lax.broadcasted_iota(jnp.int32, (BR_W, gn), 0) // (hpg * CHUNK)
    g_col = lax.broadcasted_iota(jnp.int32, (BR_W, gn), 1) // M2_STATE
    bmask = g_row == g_col
    s_row = lax.broadcasted_iota(jnp.int32, (gn, BR_W), 0) // M2_STATE
    s_col = lax.broadcasted_iota(jnp.int32, (gn, BR_W), 1) // (hpg * M2_HEADDIM)
    smask = s_row == s_col
    zero = jnp.zeros((), BF16)

    nchunk = tb // CHUNK
    order = range(nchunk - 1, -1, -1) if reverse else range(nchunk)
    for bb in range(BPS):
        xo = bb * M2_XBC
        osl = slice(bb * BR_W, (bb + 1) * BR_W)
        dt = _softplus(dt_ref[:, bb * DT_PAD:(bb + 1) * DT_PAD] + dtb_ref[...])
        cum_n = _chunk_cumsum(dt * a_ref[...], reverse)
        ex = _dot(jnp.concatenate(_split3(cum_n) + _split3(dt), axis=0), e_ref[...])
        cum_all = _sum3(ex[0:3 * tb], tb, 0)
        dt_all = _sum3(ex[3 * tb:6 * tb], tb, 0)
        for c in order:
            rs = slice(c * CHUNK, (c + 1) * CHUNK)
            xh = xs_ref[rs, xo:xo + BR_W]
            bm = xs_ref[rs, xo + BR_W:xo + BR_W + gn]
            cm = xs_ref[rs, xo + BR_W + gn:xo + BR_W + 2 * gn]
            cum = cum_all[rs, :]
            dt_e = dt_all[rs, :]
            cum_end = cum[0:1, :] if reverse else cum[CHUNK - 1:CHUNK, :]
            cum_row = jnp.sum(cum * teye, axis=0, keepdims=True)
            dt_row = jnp.sum(dt_e * teye, axis=0, keepdims=True)
            decay = jnp.exp(jnp.where(mask, cum - cum_row, -jnp.inf))
            bm8 = jnp.where(bmask, jnp.concatenate([bm] * M2_HEADS, axis=0), zero)
            scores = _dot_nt(cm, bm8)
            w = (scores * decay * dt_row).astype(BF16)
            xbd = jnp.where(bdmask, jnp.concatenate([xh] * M2_HEADS, axis=0), zero)
            st = st_ref[bb]
            y = _dot(w, xbd) + _dot(cm, st.astype(BF16)) * jnp.exp(cum)
            wx = (xh.astype(F32) * (jnp.exp(cum_end - cum) * dt_e)).astype(BF16)
            st_ref[bb] = jnp.exp(cum_end) * st + jnp.where(smask, _dot_tn(bm, wx), 0.0)
            if post:
                o_scr[rs, osl] = y
            else:
                o_ref[rs, osl] = y
            yield
        if post:
            y = of_ref[:, osl] + o_scr[:, osl] + dsk_ref[...] * xs_ref[:, xo:xo + BR_W].astype(F32)
            yz = y * _silu(z_ref[:, osl])
            ms = jnp.mean(yz * yz, axis=-1, keepdims=True)
            o_ref[:, osl] = (yz * lax.rsqrt(ms + EPS)) * nrm_ref[...]


def _ssd_part(uxs, udt, uz, of, st_in, dtb, a, e, dsk, nrm, reverse):
    tp = uxs.shape[0]
    tb = min(256, tp)
    nblk = tp // tb
    post = of is not None
    gn = M2_GROUPS * M2_STATE
    tix = (lambda i: nblk - 1 - i) if reverse else (lambda i: i)
    tile = lambda w: pl.BlockSpec((tb, BPS * w), lambda b, i: (tix(i), b))
    c2 = lambda b, i: (0, 0)
    stspec = pl.BlockSpec((BPS, gn, BR_W), lambda b, i: (b, 0, 0))
    in_specs = [tile(M2_XBC), tile(DT_PAD)]
    args = [uxs, udt]
    scratch = []
    if post:
        in_specs += [tile(BR_W), tile(BR_W)]
        args += [uz, of]
        scratch = [pltpu.VMEM((tb, BPS * BR_W), F32)]
    in_specs += [pl.BlockSpec((1, DT_PAD), c2), pl.BlockSpec((1, DT_PAD), c2),
                 pl.BlockSpec((DT_PAD, BR_W), c2)]
    args += [dtb, a, e]
    if post:
        in_specs += [pl.BlockSpec((1, BR_W), c2), pl.BlockSpec((1, BR_W), c2)]
        args += [dsk, nrm]
    in_specs.append(stspec)
    args.append(st_in)
    return dict(
        kern=functools.partial(_ssd_kernel, reverse=reverse, post=post, tb=tb),
        in_specs=in_specs, args=args, st_in=[a is st_in for a in args].index(True),
        out_shape=[jax.ShapeDtypeStruct((tp, NB * BR_W), F32), jax.ShapeDtypeStruct(st_in.shape, F32)],
        out_specs=[tile(BR_W), stspec],
        scratch=scratch)


def _s5_params(a_re, a_im, log_step, b_re, b_im, c_re, c_im):
    nl, g, n, p = b_re.shape
    gpc = g // S5_NCH
    m = S5_BLK
    ar, ai = a_re.astype(F32), a_im.astype(F32)
    step = jnp.exp(log_step.astype(F32))[..., None]
    mag = jnp.exp(ar * step)
    ab_re, ab_im = mag * jnp.cos(ai * step), mag * jnp.sin(ai * step)
    den = ar * ar + ai * ai
    co_re = ((ab_re - 1.0) * ar + ab_im * ai) / den
    co_im = (ab_im * ar - (ab_re - 1.0) * ai) / den
    br, bi = b_re.astype(F32), b_im.astype(F32)
    cr, ci = c_re.astype(F32), c_im.astype(F32)
    pr, pi = [jnp.ones_like(ab_re)], [jnp.zeros_like(ab_im)]
    for _ in range(m):
        pr, pi = pr + [pr[-1] * ab_re - pi[-1] * ab_im], pi + [pr[-1] * ab_im + pi[-1] * ab_re]
    pw_re, pw_im = jnp.stack(pr, axis=2), jnp.stack(pi, axis=2)
    steps = np.arange(m)
    e_idx = np.stack([m - 1 - steps, steps])
    g_idx = np.stack([steps + 1, m - steps])
    k_idx = np.abs(steps[:, None] - steps[None, :])
    allow = np.stack([steps[:, None] <= steps[None, :], steps[:, None] >= steps[None, :]]).astype(np.float32)

    def per_dir(pw, idx):
        return jnp.stack([pw[:, d][:, idx[d]] for d in range(2)], axis=1)

    ch = lambda t, ax: t.reshape(t.shape[:ax] + (S5_NCH, gpc) + t.shape[ax + 1:])
    hi = lax.Precision.HIGHEST

    def block_diag(t, rows, cols):
        q = np.arange(m * gpc * p)
        if cols == "n":
            tmat = np.tile(np.eye(n, dtype=np.float32), (1, gpc))
            col_g = np.arange(gpc * n) // n
        else:
            tmat = ((q[None, :] // (gpc * p) == np.arange(m * p)[:, None] // p)
                    & (q[None, :] % p == np.arange(m * p)[:, None] % p)).astype(np.float32)
            col_g = (q // p) % gpc
        r = np.arange(t.shape[3])
        row_g = (r // n) if rows == "gn" else (r // p) % gpc
        mask = (row_g[:, None] == col_g[None, :]).astype(np.float32)
        return jnp.einsum('ldcrk,kq->ldcrq', t, tmat, precision=hi) * mask

    er, ei = ch(per_dir(pw_re, e_idx), 3), ch(per_dir(pw_im, e_idx), 3)
    brc, bic = ch(br, 1)[:, None, None], ch(bi, 1)[:, None, None]
    rows_igp = lambda t: jnp.transpose(t, (0, 1, 3, 2, 4, 6, 5)).reshape(nl, 2, S5_NCH, m * gpc * p, n)
    wd = jnp.concatenate([block_diag(rows_igp(er[..., None] * brc - ei[..., None] * bic), "igp", "n"),
                          block_diag(rows_igp(er[..., None] * bic + ei[..., None] * brc), "igp", "n")],
                         axis=-1).astype(BF16)
    cc_re = cr[:, None] * co_re[:, :, :, None, :] - ci[:, None] * co_im[:, :, :, None, :]
    cc_im = cr[:, None] * co_im[:, :, :, None, :] + ci[:, None] * co_re[:, :, :, None, :]
    gr, gi = per_dir(pw_re, g_idx)[:, :, :, :, None, :], per_dir(pw_im, g_idx)[:, :, :, :, None, :]
    rows_gn = lambda t: jnp.transpose(ch(t, 3), (0, 1, 3, 4, 6, 2, 5)).reshape(nl, 2, S5_NCH, gpc * n, m * p)
    wy = jnp.concatenate([block_diag(rows_gn(cc_re[:, :, None] * gr - cc_im[:, :, None] * gi), "gn", "p"),
                          block_diag(rows_gn(-(cc_re[:, :, None] * gi + cc_im[:, :, None] * gr)), "gn", "p")],
                         axis=-2).astype(BF16)
    kr, ki = pw_re[:, :, 0:m, :, None, :], pw_im[:, :, 0:m, :, None, :]
    car = cc_re[:, :, None] * kr - cc_im[:, :, None] * ki
    cai = cc_re[:, :, None] * ki + cc_im[:, :, None] * kr
    thru = (jnp.einsum('ldkgon,lgni->ldkgio', car, br, precision=hi)
            - jnp.einsum('ldkgon,lgni->ldkgio', cai, bi, precision=hi))
    tij = thru[:, :, k_idx] * allow[None, :, :, :, None, None, None]
    wf = block_diag(jnp.transpose(ch(tij, 4), (0, 1, 4, 2, 5, 6, 3, 7)).reshape(
        nl, 2, S5_NCH, m * gpc * p, m * p), "igp", "p").astype(BF16)
    ab = jnp.concatenate([pw_re[:, :, m].reshape(nl, 2, S5_NCH, gpc * n),
                          pw_im[:, :, m].reshape(nl, 2, S5_NCH, gpc * n)], axis=-1)
    ab = jnp.broadcast_to(ab[:, :, :, None, :], (nl, 2, S5_NCH, NB, 2 * gpc * n))
    return wd, ab, wy, wf


def _lane_vec(v, d):
    out = jnp.zeros((1, DT_PAD), F32)
    return lax.dynamic_update_slice(out, v.astype(F32)[None, :], (0, M2_HEADS * d))


def _head_expand(d):
    r = jnp.arange(DT_PAD)[:, None]
    c = jnp.arange(BR_W)[None, :] // M2_HEADDIM
    return (r == c + M2_HEADS * d).astype(BF16)


def kernel(x, c, ctx, c_ctx, norm_w, w_mod, b_mod, w_in, hg_lb_logits, hg_norm, s5_a_re, s5_a_im, s5_log_step, s5_b_re, s5_b_im, s5_c_re, s5_c_im, s5_d, s5_w_glu, s5_b_glu, lru_conv_w, lru_conv_b, lru_gate_w, lru_gate_b, lru_lam, m2_conv_w, m2_conv_b, m2_dt_bias, m2_a_log, m2_d, m2_norm, w_branch, w_gate, b_gate, w_out, final_norm):
    bsz, seq, dm = x.shape
    n_ctx = ctx.shape[1]
    depth = norm_w.shape[0]
    assert bsz == NB and seq % 256 == 0 and n_ctx % CHUNK == 0

    xl, xc = x, ctx
    c_all = jnp.concatenate([c, jnp.broadcast_to(c_ctx[None, :], (NB, dm))], axis=0)
    lb_all = jnp.cumsum(jax.nn.softmax(hg_lb_logits.astype(F32), axis=0), axis=0)
    perm = _tile_perm()
    perm_t = perm.T

    o_s5 = 5 * BR_W
    o_lru = o_s5 + 2 * BR_W
    o_lz = o_lru + BR_W
    o_m2 = o_lru + 2 * BR_W
    o_dt = o_m2 + M2_XBC
    o_mz = o_dt + 2 * M2_HEADS
    gn = M2_GROUPS * M2_STATE

    s5_w = _s5_params(s5_a_re, s5_a_im, s5_log_step, s5_b_re, s5_b_im, s5_c_re, s5_c_im)
    nbh = lru_gate_w.shape[3] // LRU_HALVES
    bw = lru_gate_w.shape[-1]
    gw = (0.5 * lru_gate_w).reshape(depth, 2, 2, LRU_HALVES, nbh, bw, bw)
    lru_wg = jnp.einsum('ldgjnab,nm->ldjnagmb', gw, jnp.eye(nbh, dtype=F32)).reshape(
        depth, 2, LRU_HALVES, nbh * bw, 2 * nbh * bw).astype(BF16)
    lru_bg = jnp.transpose((0.5 * lru_gate_b).reshape(depth, 2, 2, LRU_HALVES, nbh * bw),
                           (0, 1, 3, 2, 4)).reshape(depth, 2, LRU_HALVES, 1, 2 * nbh * bw)

    for l in range(depth):
        mod = _mod_call(c_all, w_mod[l].astype(BF16), b_mod[l][None, :])
        nw = norm_w[l][None, :]
        wl = w_in[l]
        w_s5 = wl[:, o_s5:o_lru].astype(BF16)
        w_lx = wl[:, o_lru:o_lz].astype(BF16)
        w_lz = wl[:, o_lz:o_m2].astype(BF16)
        w_mx = wl[:, o_m2:o_dt].astype(BF16)
        w_bm = jnp.concatenate(
            [wl[:, 0:o_s5], jnp.pad(wl[:, o_dt:o_mz], ((0, 0), (0, DT_PAD - 2 * M2_HEADS))), wl[:, o_mz:]],
            axis=1).astype(BF16)
        wg = w_gate[l].astype(BF16)
        bg = b_gate[l][:, None, :]
        wb = w_branch[l].astype(BF16)
        wo = w_out[l].astype(BF16)

        parts = []
        for name, x3, row0, period in (("ctx", xc, NB, n_ctx), ("lat", xl, 0, CHUNK)):
            sh = mod[row0:row0 + NB, 0:dm]
            sc = mod[row0:row0 + NB, dm:2 * dm]
            gt = mod[row0:row0 + NB, 2 * dm:3 * dm]
            ub, ucx, ucz, uxs, uq, uv, uff, ufb, uaz, udt, udz = _proj_call(
                x3, nw, sc, sh, perm, perm_t, w_s5, w_lx, w_lz, lru_conv_w[l], lru_conv_b[l][None, :],
                w_mx, w_bm, (BR_W,) * 5 + (DT_PAD, BR_W), m2_conv_w[l], m2_conv_b[l][None, :],
                period // CHUNK, "proj_" + name)
            parts.append(dict(name=name, x=x3, sc=sc, sh=sh, gt=gt, uq=uq, uv=uv, uf=(uff, ufb), uaz=uaz,
                              ub=ub, ucx=ucx, ucz=ucz, uxs=uxs, udt=udt, udz=udz))

        ys = [dict(), dict()]
        for d, reverse in ((0, False), (1, True)):
            tag = "bwd" if reverse else "fwd"
            st_a = jnp.zeros((NB, HG_HEADS, HG_DK, HG_DK), F32)
            st_b = jnp.zeros((NB, 2 * (BR_W // S5_GROUP) * S5_STATE), F32)
            st_c = jnp.zeros((NB, BR_W), F32)
            st_d = jnp.zeros((NB, gn, BR_W), F32)
            for pi, p in enumerate(parts):
                nm = tag + "_" + p["name"]
                of = ys[pi] if reverse else dict(a=None, b=None, c=None, d=None)
                r = p["ub"].shape[0]
                tp = r // NB
                (ob, st_b), (oc, st_c) = _fused_call(
                    [_s5_part(p["ub"], of["b"], st_b, *s5_w, (l, d), s5_d[l][None, :],
                              s5_w_glu[l].astype(BF16), s5_b_glu[l][None, :], reverse),
                     _lru_part(p["ucx"], p["ucz"], of["c"], st_c, lru_wg, lru_bg, (l, d),
                               lru_lam[l, d][None, :], reverse)],
                    (r // TILE,), "tm_" + nm)
                (oa, st_a), (od, st_d) = _fused_call(
                    [_hgrn_part(p["uq"], p["uv"], p["uf"][d], p["uaz"], of["a"], st_a,
                                lb_all[l, d][None, :], hg_norm[l][None, :], reverse),
                     _ssd_part(p["uxs"], p["udt"], p["udz"], of["d"], st_d,
                               _lane_vec(m2_dt_bias[l, d], d),
                               _lane_vec(-jnp.exp(m2_a_log[l, d].astype(F32)), d), _head_expand(d),
                               jnp.repeat(m2_d[l].astype(F32), M2_HEADDIM)[None, :],
                               m2_norm[l][None, :], reverse)],
                    (NB // BPS, tp // min(256, tp)), "bm_" + nm)
                ys[pi] = dict(a=oa, b=ob, c=oc, d=od)

        last = l == depth - 1
        for pi, p in enumerate(parts):
            if last and p["name"] == "ctx":
                continue
            y = ys[pi]
            out = _merge_call(p["x"], (y["a"], y["b"], y["c"], y["d"]), nw, p["sc"], p["sh"], p["gt"],
                              perm, wg, bg, wb, wo, final_norm[None, :], last, "merge_" + p["name"])
            if p["name"] == "ctx":
                xc = out
            else:
                xl = out
    return xl
```
